```python
import jax
import jax.numpy as jnp
from jax import lax
import numpy as np

D_MODEL = 4096
BATCH = 16
SEQ = 256
DEPTH = 2
DEC_BATCH = 4
DEC_SEQ = 1024
PAST_LEN = 512

GRID_W = 64
MLA_HEADS = 16
QK_NOPE = 128
QK_ROPE = 64
V_HEAD = 128
Q_LORA = 1024
KV_LORA = 512
GLA_HEADS = 8
GLA_DK = 128
GLA_DV = 256
GATE_RANK = 16
GATE_TAU = 16.0
CHUNK = 64
D_FF = 4 * D_MODEL
ROPE_BASE = 10000.0
EPS = 1e-6
Q_BLOCK = 128

MLA_OUT = MLA_HEADS * V_HEAD
GLA_QK = GLA_HEADS * GLA_DK
GLA_OUT = GLA_HEADS * GLA_DV
MIX_WIDTH = MLA_OUT + GLA_OUT
OFF_KV = Q_LORA
OFF_GQ = OFF_KV + KV_LORA + QK_ROPE
OFF_GK = OFF_GQ + GLA_QK
OFF_GV = OFF_GK + GLA_QK
OFF_GATE = OFF_GV + GLA_OUT
OFF_OG = OFF_GATE + 2 * GATE_RANK
N_IN = OFF_OG + GLA_OUT

kernel_name = 'hybrid_mla_gla_diffusion_step'


def rmsnorm(x, g):
    xf = x.astype(jnp.float32)
    y = xf * lax.rsqrt(jnp.mean(xf * xf, axis=-1, keepdims=True) + EPS)
    return (y * g.astype(jnp.float32)).astype(x.dtype)


def axial_rope(n_tok):
    rows = n_tok // GRID_W
    row = jnp.repeat(jnp.arange(rows, dtype=jnp.float32), GRID_W)
    col = jnp.tile(jnp.arange(GRID_W, dtype=jnp.float32), rows)
    n_freq = QK_ROPE // 4
    inv = jnp.power(ROPE_BASE, -jnp.arange(n_freq, dtype=jnp.float32) / n_freq)
    ang = jnp.concatenate([row[:, None] * inv, col[:, None] * inv], axis=-1)
    return jnp.cos(ang), jnp.sin(ang)


def apply_rope(x, cos, sin):
    half = QK_ROPE // 2
    x1 = x[..., :half].astype(jnp.float32)
    x2 = x[..., half:].astype(jnp.float32)
    return jnp.concatenate([x1 * cos - x2 * sin, x1 * sin + x2 * cos], axis=-1).astype(x.dtype)


def block_attention(q, k, v):
    B, Tq, H, dq = q.shape
    nb = Tq // Q_BLOCK
    qb = q.reshape(B, nb, Q_BLOCK, H, dq).transpose(1, 0, 2, 3, 4)
    scale = dq ** -0.5

    def one(qi):
        s = jnp.einsum('bqhd,bkhd->bhqk', qi, k, preferred_element_type=jnp.float32) * scale
        p = jax.nn.softmax(s, axis=-1)
        return jnp.einsum('bhqk,bkhd->bqhd', p.astype(v.dtype), v)

    o = lax.map(one, qb)
    return o.transpose(1, 0, 2, 3, 4).reshape(B, Tq, H, v.shape[-1])


def gla_scan(q, k, v, g, s0):
    B, T, H, dk = q.shape
    dv = v.shape[-1]
    n = T // CHUNK

    def to_chunks(a):
        return a.reshape(B, n, CHUNK, H, a.shape[-1]).transpose(1, 0, 3, 2, 4)

    mask = jnp.tril(jnp.ones((CHUNK, CHUNK), dtype=bool))

    def step(S, inp):
        qi, ki, vi, gi = inp
        b = jnp.cumsum(gi.astype(jnp.float32), axis=2)
        diff = b[:, :, :, None, :] - b[:, :, None, :, :]
        decay = jnp.exp(jnp.where(mask[:, :, None], diff, -jnp.inf))
        A = jnp.einsum('bhid,bhjd,bhijd->bhij', qi, ki, decay)
        o = jnp.einsum('bhij,bhjv->bhiv', A, vi) + jnp.einsum('bhid,bhdv->bhiv', qi * jnp.exp(b), S)
        b_last = b[:, :, -1:, :]
        S_new = jnp.exp(b_last[:, :, 0, :])[..., None] * S + jnp.einsum('bhjd,bhjv->bhdv', ki * jnp.exp(b_last - b), vi)
        return S_new, o

    S, o = lax.scan(step, s0.astype(jnp.float32), (to_chunks(q), to_chunks(k), to_chunks(v), to_chunks(g)))
    o = o.transpose(1, 0, 3, 2, 4).reshape(B, T, H, dv)
    return o, S


def mixer(h, P, l, rope, ctx_ckv, ctx_krope, s0_fwd, s0_bwd):
    B, T, _ = h.shape
    proj = h @ P['w_in'][l]
    q_lat, kv_lat, gq, gk, gv, g_lr, og = jnp.split(proj, [OFF_KV, OFF_GQ, OFF_GK, OFF_GV, OFF_GATE, OFF_OG], axis=-1)

    q = (rmsnorm(q_lat, P['q_a_norm'][l]) @ P['w_qb'][l]).reshape(B, T, MLA_HEADS, QK_NOPE + QK_ROPE)
    q_nope = rmsnorm(q[..., :QK_NOPE], P['q_norm_nope'][l])
    q_rope = rmsnorm(q[..., QK_NOPE:], P['q_norm_rope'][l])
    ckv = rmsnorm(kv_lat[..., :KV_LORA], P['kv_a_norm'][l])
    krope = rmsnorm(kv_lat[..., KV_LORA:], P['k_norm_rope'][l])
    if rope is None:
        keys_ckv, keys_krope = ckv, krope
    else:
        cos, sin = rope
        q_rope = apply_rope(q_rope, cos[:, None, :], sin[:, None, :])
        keys_ckv = jnp.concatenate([ckv, ctx_ckv.astype(ckv.dtype)], axis=1)
        keys_krope = jnp.concatenate([apply_rope(krope, cos, sin), ctx_krope.astype(krope.dtype)], axis=1)
    Tk = keys_ckv.shape[1]
    kv = (keys_ckv @ P['w_kvb'][l]).reshape(B, Tk, MLA_HEADS, QK_NOPE + V_HEAD)
    k_nope = rmsnorm(kv[..., :QK_NOPE], P['k_norm_nope'][l])
    v = kv[..., QK_NOPE:]
    k = jnp.concatenate([k_nope, jnp.broadcast_to(keys_krope[:, :, None, :], (B, Tk, MLA_HEADS, QK_ROPE))], axis=-1)
    o_mla = block_attention(jnp.concatenate([q_nope, q_rope], axis=-1), k, v).reshape(B, T, MLA_OUT)
    o_mla = rmsnorm(o_mla, P['mla_out_norm'][l])

    gq = gq.reshape(B, T, GLA_HEADS, GLA_DK) * (GLA_DK ** -0.5)
    gk = gk.reshape(B, T, GLA_HEADS, GLA_DK)
    gv = gv.reshape(B, T, GLA_HEADS, GLA_DV)
    g_f = jax.nn.log_sigmoid((g_lr[..., :GATE_RANK] @ P['w_gf2'][l] + P['b_gf'][l]).astype(jnp.float32)) / GATE_TAU
    g_b = jax.nn.log_sigmoid((g_lr[..., GATE_RANK:] @ P['w_gb2'][l] + P['b_gb'][l]).astype(jnp.float32)) / GATE_TAU
    g_f = g_f.reshape(B, T, GLA_HEADS, GLA_DK)
    g_b = g_b.reshape(B, T, GLA_HEADS, GLA_DK)
    if s0_fwd is None:
        s0_fwd = jnp.zeros((B, GLA_HEADS, GLA_DK, GLA_DV), jnp.float32)
        s0_bwd = jnp.zeros((B, GLA_HEADS, GLA_DK, GLA_DV), jnp.float32)
    o_f, s_f = gla_scan(gq, gk, gv, g_f, s0_fwd)
    flip = lambda a: jnp.flip(a, axis=1)
    o_b, s_b = gla_scan(flip(gq), flip(gk), flip(gv), flip(g_b), s0_bwd)
    o_gla = (o_f + flip(o_b)).astype(h.dtype)
    o_gla = rmsnorm(o_gla, P['gla_norm'][l]).reshape(B, T, GLA_OUT) * jax.nn.silu(og)

    out = jnp.concatenate([o_mla, o_gla], axis=-1) @ P['w_o'][l]
    return out, (ckv, krope, s_f, s_b)


def trunk_layer(x, cvec, P, l, rope, ctx_ckv, ctx_krope, s0_fwd, s0_bwd):
    mod = (jax.nn.silu(cvec) @ P['w_ada'][l] + P['b_ada'][l])[:, None, :]
    sh1, sc1, gt1, sh2, sc2, gt2 = jnp.split(mod, 6, axis=-1)
    h = rmsnorm(x, P['norm1'][l]) * (1 + sc1) + sh1
    mix, ctx_out = mixer(h, P, l, rope, ctx_ckv, ctx_krope, s0_fwd, s0_bwd)
    x = x + gt1 * mix
    h = rmsnorm(x, P['norm2'][l]) * (1 + sc2) + sh2
    f = jnp.square(jax.nn.relu(h @ P['w_up'][l])) @ P['w_down'][l]
    x = x + gt2 * f
    return x, ctx_out


def setup_inputs(seed: int = 0) -> dict:
    key = jax.random.key(seed)
    ks = iter(jax.random.split(key, 40))

    def nrm(shape, scale=1.0):
        return jax.random.normal(next(ks), shape, jnp.float32) * scale

    def gain(shape):
        return 1.0 + nrm(shape, 0.02)

    return {
        'x_prompt': nrm((BATCH, SEQ, D_MODEL)),
        'x_sample': nrm((DEC_BATCH, DEC_SEQ, D_MODEL)),
        'c': nrm((DEC_BATCH, D_MODEL)),
        'cache_ckv': nrm((DEC_BATCH, DEPTH, PAST_LEN, KV_LORA)),
        'cache_krope': nrm((DEC_BATCH, DEPTH, PAST_LEN, QK_ROPE)),
        'state_gla_fwd': nrm((DEC_BATCH, DEPTH, GLA_HEADS, GLA_DK, GLA_DV)),
        'state_gla_bwd': nrm((DEC_BATCH, DEPTH, GLA_HEADS, GLA_DK, GLA_DV)),
        'c_ctx': nrm((D_MODEL,)),
        'w_ada': nrm((DEPTH, D_MODEL, 6 * D_MODEL), 0.5 * D_MODEL ** -0.5),
        'b_ada': nrm((DEPTH, 6 * D_MODEL), 0.01),
        'norm1': gain((DEPTH, D_MODEL)),
        'norm2': gain((DEPTH, D_MODEL)),
        'w_in': nrm((DEPTH, D_MODEL, N_IN), D_MODEL ** -0.5),
        'q_a_norm': gain((DEPTH, Q_LORA)),
        'w_qb': nrm((DEPTH, Q_LORA, MLA_HEADS * (QK_NOPE + QK_ROPE)), Q_LORA ** -0.5),
        'kv_a_norm': gain((DEPTH, KV_LORA)),
        'w_kvb': nrm((DEPTH, KV_LORA, MLA_HEADS * (QK_NOPE + V_HEAD)), KV_LORA ** -0.5),
        'q_norm_nope': gain((DEPTH, QK_NOPE)),
        'q_norm_rope': gain((DEPTH, QK_ROPE)),
        'k_norm_nope': gain((DEPTH, QK_NOPE)),
        'k_norm_rope': gain((DEPTH, QK_ROPE)),
        'w_gf2': nrm((DEPTH, GATE_RANK, GLA_QK), GATE_RANK ** -0.5),
        'b_gf': nrm((DEPTH, GLA_QK), 0.1),
        'w_gb2': nrm((DEPTH, GATE_RANK, GLA_QK), GATE_RANK ** -0.5),
        'b_gb': nrm((DEPTH, GLA_QK), 0.1),
        'gla_norm': gain((DEPTH, GLA_DV)),
        'mla_out_norm': gain((DEPTH, MLA_OUT)),
        'w_o': nrm((DEPTH, MIX_WIDTH, D_MODEL), MIX_WIDTH ** -0.5),
        'w_up': nrm((DEPTH, D_MODEL, D_FF), D_MODEL ** -0.5),
        'w_down': nrm((DEPTH, D_FF, D_MODEL), D_FF ** -0.5),
    }


def reference(x_prompt, x_sample, c, cache_ckv, cache_krope, state_gla_fwd, state_gla_bwd, c_ctx,
              w_ada, b_ada, norm1, norm2, w_in, q_a_norm, w_qb, kv_a_norm, w_kvb,
              q_norm_nope, q_norm_rope, k_norm_nope, k_norm_rope, w_gf2, b_gf, w_gb2, b_gb,
              gla_norm, mla_out_norm, w_o, w_up, w_down):
    P = {
        'w_ada': w_ada, 'b_ada': b_ada, 'norm1': norm1, 'norm2': norm2, 'w_in': w_in,
        'q_a_norm': q_a_norm, 'w_qb': w_qb, 'kv_a_norm': kv_a_norm, 'w_kvb': w_kvb,
        'q_norm_nope': q_norm_nope, 'q_norm_rope': q_norm_rope,
        'k_norm_nope': k_norm_nope, 'k_norm_rope': k_norm_rope,
        'w_gf2': w_gf2, 'b_gf': b_gf, 'w_gb2': w_gb2, 'b_gb': b_gb,
        'gla_norm': gla_norm, 'mla_out_norm': mla_out_norm, 'w_o': w_o, 'w_up': w_up, 'w_down': w_down,
    }

    y_prompt = x_prompt
    ckvs, kropes, sfs, sbs = [], [], [], []
    for l in range(DEPTH):
        y_prompt, (ckv, krope, s_f, s_b) = trunk_layer(y_prompt, c_ctx[None, :], P, l, None, None, None, None, None)
        ckvs.append(ckv)
        kropes.append(krope)
        sfs.append(s_f)
        sbs.append(s_b)
    new_ckv = jnp.stack(ckvs, axis=1)
    new_krope = jnp.stack(kropes, axis=1)
    new_state_fwd = jnp.stack(sfs, axis=1)
    new_state_bwd = jnp.stack(sbs, axis=1)

    rope = axial_rope(x_sample.shape[1])
    y_sample = x_sample
    for l in range(DEPTH):
        y_sample, _ = trunk_layer(y_sample, c, P, l, rope, cache_ckv[:, l], cache_krope[:, l],
                                  state_gla_fwd[:, l], state_gla_bwd[:, l])

    return (y_prompt, y_sample, new_ckv, new_krope, new_state_fwd, new_state_bwd)
```

```python
import functools

import jax
import jax.numpy as jnp
from jax import lax
from jax.experimental import pallas as pl
from jax.experimental.pallas import tpu as pltpu

D_MODEL = 4096
BATCH = 16
SEQ = 256
DEPTH = 2
DEC_BATCH = 4
DEC_SEQ = 1024
PAST_LEN = 512
GRID_W = 64
MLA_HEADS = 16
QK_NOPE = 128
QK_ROPE = 64
V_HEAD = 128
Q_LORA = 1024
KV_LORA = 512
GLA_HEADS = 8
GLA_DK = 128
GLA_DV = 256
GATE_RANK = 16
GATE_TAU = 16.0
CHUNK = 64
D_FF = 4 * D_MODEL
ROPE_BASE = 10000.0
EPS = 1e-6

MLA_OUT = MLA_HEADS * V_HEAD
GLA_QK = GLA_HEADS * GLA_DK
GLA_OUT = GLA_HEADS * GLA_DV
OFF_KV = Q_LORA
OFF_GQ = OFF_KV + KV_LORA + QK_ROPE
OFF_GK = OFF_GQ + GLA_QK
OFF_GV = OFF_GK + GLA_QK
OFF_GATE = OFF_GV + GLA_OUT
OFF_OG = OFF_GATE + 2 * GATE_RANK
N_IN = OFF_OG + GLA_OUT

F32 = jnp.float32
BF16 = jnp.bfloat16

V7X_LANES = 128
V7X_SCOPED_VMEM_BYTES = 60000 * 1024

N_PROMPT = BATCH * SEQ
N_SAMPLE = DEC_BATCH * DEC_SEQ
N_TOK = N_PROMPT + N_SAMPLE
N_CACHE = DEC_BATCH * PAST_LEN
N_MOD_ROWS = 8

P_KV = 1024
P_KROPE = P_KV + KV_LORA
P_GATE = P_KROPE + V7X_LANES
P_GQ = 2048
P_GK = P_GQ + GLA_QK
P_GV = P_GK + GLA_QK
P_OG = P_GV + GLA_OUT
P_TOTAL = P_OG + GLA_OUT
QK_PAD = 2 * V7X_LANES

ATTN_SCALE = float((QK_NOPE + QK_ROPE) ** -0.5)
GLA_SCALE = float(GLA_DK ** -0.5)


def _cparams(sem, vmem_bytes):
    return pltpu.CompilerParams(dimension_semantics=sem,
                                vmem_limit_bytes=min(int(vmem_bytes), V7X_SCOPED_VMEM_BYTES))


def _mod_row(row_block, rows_per_block):
    return jnp.maximum(row_block * rows_per_block // DEC_SEQ - (N_PROMPT // DEC_SEQ - 1), 0)


def _rms(x, g, n):
    ms = jnp.sum(x * x, axis=-1, keepdims=True) * (1.0 / n)
    return x * lax.rsqrt(ms + EPS) * g


def _ada_kernel(c_ref, w_ref, b_ref, o_ref):
    k = pl.program_id(2)

    @pl.when(k == 0)
    def _():
        o_ref[...] = jnp.broadcast_to(b_ref[...], o_ref.shape)

    c = c_ref[...]
    s = (c / (1.0 + jnp.exp(-c))).astype(BF16)
    o_ref[...] += jnp.dot(s, w_ref[...].astype(BF16), preferred_element_type=F32)


def _ada(cvec, w_ada, b_ada):
    tk, tn = 1024, 2048
    n_out = 6 * D_MODEL
    return pl.pallas_call(
        _ada_kernel,
        out_shape=jax.ShapeDtypeStruct((DEPTH, N_MOD_ROWS, n_out), F32),
        grid=(DEPTH, n_out // tn, D_MODEL // tk),
        in_specs=[
            pl.BlockSpec((N_MOD_ROWS, tk), lambda l, j, k: (0, k)),
            pl.BlockSpec((None, tk, tn), lambda l, j, k: (l, k, j)),
            pl.BlockSpec((None, 1, tn), lambda l, j, k: (l, 0, j)),
        ],
        out_specs=pl.BlockSpec((None, N_MOD_ROWS, tn), lambda l, j, k: (l, 0, j)),
        compiler_params=_cparams(("parallel", "parallel", "arbitrary"), 3 * tk * tn * 4),
        name="ada_mod",
    )(cvec, w_ada, b_ada.reshape(DEPTH, 1, n_out))


def _normmod_kernel(x_ref, g_ref, mod_ref, o_ref, *, shift_idx, scale_idx):
    y = _rms(x_ref[...], g_ref[...], D_MODEL)
    sc = mod_ref[0, scale_idx:scale_idx + 1, :]
    sh = mod_ref[0, shift_idx:shift_idx + 1, :]
    o_ref[...] = (y * (1.0 + sc) + sh).astype(BF16)


def _normmod(x, g, mod, shift_idx, scale_idx):
    tm = 256
    return pl.pallas_call(
        functools.partial(_normmod_kernel, shift_idx=shift_idx, scale_idx=scale_idx),
        out_shape=jax.ShapeDtypeStruct((N_TOK, D_MODEL), BF16),
        grid=(N_TOK // tm,),
        in_specs=[
            pl.BlockSpec((tm, D_MODEL), lambda i: (i, 0)),
            pl.BlockSpec((1, D_MODEL), lambda i: (0, 0)),
            pl.BlockSpec((1, 6, D_MODEL), lambda i: (_mod_row(i, tm), 0, 0)),
        ],
        out_specs=pl.BlockSpec((tm, D_MODEL), lambda i: (i, 0)),
        compiler_params=_cparams(("parallel",), 4 * tm * D_MODEL * 4),
        name="norm_mod",
    )(x, g.reshape(1, D_MODEL), mod)


def _accumulate(a_ref, b_ref, acc_ref):
    k = pl.program_id(2)
    p = jnp.dot(a_ref[...].astype(BF16), b_ref[...].astype(BF16), preferred_element_type=F32)

    @pl.when(k == 0)
    def _():
        acc_ref[...] = p

    @pl.when(k > 0)
    def _():
        acc_ref[...] += p


def _mm_plain_kernel(a_ref, b_ref, o_ref, acc_ref, *, nk, relu2):
    _accumulate(a_ref, b_ref, acc_ref)

    @pl.when(pl.program_id(2) == nk - 1)
    def _():
        r = acc_ref[...]
        if relu2:
            r = jnp.square(jnp.maximum(r, 0.0))
        o_ref[...] = r.astype(o_ref.dtype)


def _mm_resid_kernel(a_ref, b_ref, res_ref, mod_ref, o_ref, acc_ref, *, nk, gate_idx):
    _accumulate(a_ref, b_ref, acc_ref)

    @pl.when(pl.program_id(2) == nk - 1)
    def _():
        gate = mod_ref[0, gate_idx:gate_idx + 1, :]
        o_ref[...] = res_ref[...] + gate * acc_ref[...]


def _mm_vmem(tm, tn, tk, a_bytes, b_bytes, o_bytes, extra=0):
    blocks = 2 * (tm * tk * a_bytes + tk * tn * b_bytes + tm * tn * o_bytes)
    temps = tm * tn * 4 * 2 + tk * tn * 2 + tm * tk * 2
    return blocks + temps + extra


def _matmul(a, b, *, tm, tn, tk, out_dtype, relu2=False, name):
    m, kdim = a.shape
    n = b.shape[1]
    nk = kdim // tk
    return pl.pallas_call(
        functools.partial(_mm_plain_kernel, nk=nk, relu2=relu2),
        out_shape=jax.ShapeDtypeStruct((m, n), out_dtype),
        grid=(m // tm, n // tn, nk),
        in_specs=[
            pl.BlockSpec((tm, tk), lambda i, j, k: (i, k)),
            pl.BlockSpec((tk, tn), lambda i, j, k: (k, j)),
        ],
        out_specs=pl.BlockSpec((tm, tn), lambda i, j, k: (i, j)),
        scratch_shapes=[pltpu.VMEM((tm, tn), F32)],
        compiler_params=_cparams(
            ("parallel", "parallel", "arbitrary"),
            _mm_vmem(tm, tn, tk, a.dtype.itemsize, b.dtype.itemsize, jnp.dtype(out_dtype).itemsize)),
        name=name,
    )(a, b)


def _matmul_resid(a, b, res, mod, gate_idx, *, tm, tn, tk, name):
    m, kdim = a.shape
    n = b.shape[1]
    nk = kdim // tk
    return pl.pallas_call(
        functools.partial(_mm_resid_kernel, nk=nk, gate_idx=gate_idx),
        out_shape=jax.ShapeDtypeStruct((m, n), F32),
        grid=(m // tm, n // tn, nk),
        in_specs=[
            pl.BlockSpec((tm, tk), lambda i, j, k: (i, k)),
            pl.BlockSpec((tk, tn), lambda i, j, k: (k, j)),
            pl.BlockSpec((tm, tn), lambda i, j, k: (i, j)),
            pl.BlockSpec((1, 6, tn), lambda i, j, k: (_mod_row(i, tm), 0, j)),
        ],
        out_specs=pl.BlockSpec((tm, tn), lambda i, j, k: (i, j)),
        scratch_shapes=[pltpu.VMEM((tm, tn), F32)],
        compiler_params=_cparams(
            ("parallel", "parallel", "arbitrary"),
            _mm_vmem(tm, tn, tk, a.dtype.itemsize, b.dtype.itemsize, 4, extra=2 * tm * tn * 4)),
        name=name,
    )(a, b, res, mod)


def _rope128(r, cos, sa, sb):
    return r * cos + pltpu.roll(r, 96, 1) * sa + pltpu.roll(r, 32, 1) * sb


def _q_kernel(ql_ref, ga_ref, w_ref, gn_ref, gr_ref, cos_ref, sa_ref, sb_ref, o_ref, qn_ref, *, heads_per_tile):
    @pl.when(pl.program_id(1) == 0)
    def _():
        qn_ref[...] = _rms(ql_ref[...], ga_ref[...], Q_LORA).astype(BF16)

    acc = jnp.dot(qn_ref[...], w_ref[...].astype(BF16), preferred_element_type=F32)
    cos, sa, sb = cos_ref[...], sa_ref[...], sb_ref[...]
    for hh in range(heads_per_tile):
        c0 = hh * QK_PAD
        nope = acc[:, c0:c0 + QK_NOPE]
        rope = acc[:, c0 + QK_NOPE:c0 + QK_PAD]
        nope = _rms(nope, gn_ref[...], QK_NOPE) * ATTN_SCALE
        rope = _rope128(_rms(rope, gr_ref[...], QK_ROPE), cos, sa, sb) * ATTN_SCALE
        o_ref[:, c0:c0 + QK_NOPE] = nope.astype(BF16)
        o_ref[:, c0 + QK_NOPE:c0 + QK_PAD] = rope.astype(BF16)


def _rope_specs(tm, nargs):
    blocks_per_seq = DEC_SEQ // tm

    def idx(i, *_):
        return (jnp.minimum(i * tm // N_PROMPT, 1), i % blocks_per_seq, 0)

    return [pl.BlockSpec((None, tm, V7X_LANES), idx) for _ in range(nargs)]


def _q_proj(proj, ga, w_qb_p, gn, gr128, rope_tabs):
    tm, tn = 512, 1024
    n = MLA_HEADS * QK_PAD
    return pl.pallas_call(
        functools.partial(_q_kernel, heads_per_tile=tn // QK_PAD),
        out_shape=jax.ShapeDtypeStruct((N_TOK, n), BF16),
        grid=(N_TOK // tm, n // tn),
        in_specs=[
            pl.BlockSpec((tm, Q_LORA), lambda i, j: (i, 0)),
            pl.BlockSpec((1, Q_LORA), lambda i, j: (0, 0)),
            pl.BlockSpec((Q_LORA, tn), lambda i, j: (0, j)),
            pl.BlockSpec((1, QK_NOPE), lambda i, j: (0, 0)),
            pl.BlockSpec((1, V7X_LANES), lambda i, j: (0, 0)),
        ] + _rope_specs(tm, 3),
        out_specs=pl.BlockSpec((tm, tn), lambda i, j: (i, j)),
        scratch_shapes=[pltpu.VMEM((tm, Q_LORA), BF16)],
        compiler_params=_cparams(
            ("parallel", "arbitrary"),
            2 * (tm * Q_LORA * 4 + Q_LORA * tn * 2 + tm * tn * 2) + tm * Q_LORA * 2 + 3 * tm * tn * 4),
        name="mla_q_proj",
    )(proj, ga, w_qb_p, gn, gr128, *rope_tabs)


def _kvnorm_kernel(c_ref, r_ref, gc_ref, gk_ref, cos_ref, sa_ref, sb_ref, ckv_ref, kr_ref, krr_ref):
    ckv_ref[...] = _rms(c_ref[...], gc_ref[...], KV_LORA)
    rn = _rms(r_ref[...], gk_ref[...], QK_ROPE)
    kr_ref[...] = rn
    krr_ref[...] = _rope128(rn, cos_ref[...], sa_ref[...], sb_ref[...]).astype(BF16)


def _kv_norm(proj, gc, gk128, rope_tabs):
    tm = 512
    return pl.pallas_call(
        _kvnorm_kernel,
        out_shape=(jax.ShapeDtypeStruct((N_TOK, KV_LORA), F32),
                   jax.ShapeDtypeStruct((N_TOK, V7X_LANES), F32),
                   jax.ShapeDtypeStruct((N_TOK, V7X_LANES), BF16)),
        grid=(N_TOK // tm,),
        in_specs=[
            pl.BlockSpec((tm, KV_LORA), lambda i: (i, P_KV // KV_LORA)),
            pl.BlockSpec((tm, V7X_LANES), lambda i: (i, P_KROPE // V7X_LANES)),
            pl.BlockSpec((1, KV_LORA), lambda i: (0, 0)),
            pl.BlockSpec((1, V7X_LANES), lambda i: (0, 0)),
        ] + _rope_specs(tm, 3),
        out_specs=(pl.BlockSpec((tm, KV_LORA), lambda i: (i, 0)),
                   pl.BlockSpec((tm, V7X_LANES), lambda i: (i, 0)),
                   pl.BlockSpec((tm, V7X_LANES), lambda i: (i, 0))),
        compiler_params=_cparams(("parallel",), 8 * tm * (KV_LORA + 4 * V7X_LANES) * 4),
        name="mla_kv_norm",
    )(proj, proj, gc, gk128, *rope_tabs)


def _kvup_kernel(c_ref, w_ref, gn_ref, o_ref, *, heads_per_tile, k_tiles):
    acc = jnp.dot(c_ref[...].astype(BF16), w_ref[...].astype(BF16), preferred_element_type=F32)

    @pl.when(pl.program_id(1) < k_tiles)
    def _():
        for hh in range(heads_per_tile):
            c0 = hh * QK_NOPE
            o_ref[:, c0:c0 + QK_NOPE] = _rms(acc[:, c0:c0 + QK_NOPE], gn_ref[...], QK_NOPE).astype(BF16)

    @pl.when(pl.program_id(1) >= k_tiles)
    def _():
        o_ref[...] = acc.astype(BF16)


def _kv_up(ckv_all, w_kvb_p, gn):
    tm, tn = 512, 1024
    m = ckv_all.shape[0]
    n = 2 * MLA_OUT
    return pl.pallas_call(
        functools.partial(_kvup_kernel, heads_per_tile=tn // QK_NOPE, k_tiles=MLA_OUT // tn),
        out_shape=jax.ShapeDtypeStruct((m, n), BF16),
        grid=(m // tm, n // tn),
        in_specs=[
            pl.BlockSpec((tm, KV_LORA), lambda i, j: (i, 0)),
            pl.BlockSpec((KV_LORA, tn), lambda i, j: (0, j)),
            pl.BlockSpec((1, QK_NOPE), lambda i, j: (0, 0)),
        ],
        out_specs=pl.BlockSpec((tm, tn), lambda i, j: (i, j)),
        compiler_params=_cparams(
            ("parallel", "parallel"),
            2 * (tm * KV_LORA * 4 + KV_LORA * tn * 2 + tm * tn * 2) + 3 * tm * tn * 4),
        name="mla_kv_up",
    )(ckv_all, w_kvb_p, gn)


def _attn_kernel(*refs, n_seg):
    q_ref = refs[0]
    seg_refs = [refs[1 + 3 * s:4 + 3 * s] for s in range(n_seg)]
    g_ref = refs[1 + 3 * n_seg]
    o_ref = refs[2 + 3 * n_seg]
    oacc_ref = refs[3 + 3 * n_seg]
    for h in range(MLA_HEADS):
        qh = q_ref[:, h * QK_PAD:(h + 1) * QK_PAD]
        scores = []
        for kn_ref, kr_ref, _ in seg_refs:
            kh = jnp.concatenate([kn_ref[:, h * QK_NOPE:(h + 1) * QK_NOPE], kr_ref[...]], axis=1)
            scores.append(lax.dot_general(qh, kh, (((1,), (1,)), ((), ())), preferred_element_type=F32))
        m = scores[0].max(axis=-1, keepdims=True)
        for s in scores[1:]:
            m = jnp.maximum(m, s.max(axis=-1, keepdims=True))
        denom = None
        out = None
        for s, (_, _, v_ref) in zip(scores, seg_refs):
            p = jnp.exp(s - m)
            ps = p.sum(axis=-1, keepdims=True)
            pv = jnp.dot(p.astype(BF16), v_ref[:, h * V_HEAD:(h + 1) * V_HEAD], preferred_element_type=F32)
            denom = ps if denom is None else denom + ps
            out = pv if out is None else out + pv
        oacc_ref[:, h * V_HEAD:(h + 1) * V_HEAD] = out / denom
    o_ref[...] = _rms(oacc_ref[...], g_ref[...], MLA_OUT).astype(BF16)


def _attention(q, kv, kr, g_out, *, n_batch, t_q, segs, q_row0):
    tq = 256
    qb = t_q // tq
    in_specs = [pl.BlockSpec((tq, MLA_HEADS * QK_PAD), lambda b, i: (q_row0 // tq + b * qb + i, 0))]
    args = [q]
    vmem = 2 * tq * MLA_HEADS * QK_PAD * 2
    for rows, row0 in segs:
        blk0 = row0 // rows
        in_specs += [
            pl.BlockSpec((rows, MLA_OUT), lambda b, i, blk0=blk0: (blk0 + b, 0)),
            pl.BlockSpec((rows, V7X_LANES), lambda b, i, blk0=blk0: (blk0 + b, 0)),
            pl.BlockSpec((rows, MLA_OUT), lambda b, i, blk0=blk0: (blk0 + b, 1)),
        ]
        args += [kv, kr, kv]
        vmem += 2 * rows * (2 * MLA_OUT + V7X_LANES) * 2 + 12 * tq * rows * 4
    in_specs.append(pl.BlockSpec((1, MLA_OUT), lambda b, i: (0, 0)))
    args.append(g_out)
    vmem += 4 * tq * MLA_OUT * 4
    return pl.pallas_call(
        functools.partial(_attn_kernel, n_seg=len(segs)),
        out_shape=jax.ShapeDtypeStruct((n_batch * t_q, MLA_OUT), BF16),
        grid=(n_batch, qb),
        in_specs=in_specs,
        out_specs=pl.BlockSpec((tq, MLA_OUT), lambda b, i: (b * qb + i, 0)),
        scratch_shapes=[pltpu.VMEM((tq, MLA_OUT), F32)],
        compiler_params=_cparams(("parallel", "arbitrary"), vmem),
        name="mla_attention_%d" % t_q,
    )(*args)


def _log_sigmoid(x):
    return -(jnp.maximum(-x, 0.0) + jnp.log1p(jnp.exp(-jnp.abs(x))))


def _gla_kernel(q_ref, k_ref, v_ref, glr_ref, wgf_ref, wgb_ref, bgf_ref, bgb_ref, s0f_ref, s0b_ref,
                og_ref, gn_ref, o_ref, sf_ref, sb_ref, gf_s, gb_s, oacc_s, *, n_chunks):
    glr = glr_ref[...].astype(BF16)

    def gate(w_ref, b_ref):
        x = jnp.dot(glr, w_ref[...].astype(BF16), preferred_element_type=F32) + b_ref[...]
        return _log_sigmoid(x) * (1.0 / GATE_TAU)

    gf_s[...] = gate(wgf_ref, bgf_ref)
    gb_s[...] = gate(wgb_ref, bgb_ref)

    row = lax.broadcasted_iota(jnp.int32, (CHUNK, CHUNK), 0)
    col = lax.broadcasted_iota(jnp.int32, (CHUNK, CHUNK), 1)
    nt = (((1,), (1,)), ((), ()))
    tn = (((0,), (0,)), ((), ()))

    def chunk(n, st, g_s, keep, end_idx, mid_idx):
        sl = pl.ds(pl.multiple_of(n * CHUNK, CHUNK), CHUNK)
        g = g_s[sl, :]
        q = q_ref[sl, :] * GLA_SCALE
        k = k_ref[sl, :]
        v = v_ref[sl, :].astype(BF16)
        b = jnp.dot(jnp.where(keep, 1.0, 0.0), g, precision=lax.Precision.HIGHEST, preferred_element_type=F32)
        b_end = b[end_idx:end_idx + 1, :]
        b_mid = b[mid_idx:mid_idx + 1, :]
        qe = (q * jnp.exp(b)).astype(BF16)
        qa = (q * jnp.exp(b - b_mid)).astype(BF16)
        ka = (k * jnp.exp(b_mid - b)).astype(BF16)
        kd = (k * jnp.exp(b_end - b)).astype(BF16)
        a = lax.dot_general(qa, ka, nt, preferred_element_type=F32)
        a = jnp.where(keep, a, 0.0).astype(BF16)
        o = (jnp.dot(a, v, preferred_element_type=F32)
             + lax.dot_general(qe, st.astype(BF16), nt, preferred_element_type=F32))
        st_new = st * jnp.exp(b_end) + lax.dot_general(v, kd, tn, preferred_element_type=F32)
        return sl, o, st_new

    keep_f = col <= row
    keep_b = col >= row

    def fwd(n, st):
        sl, o, st = chunk(n, st, gf_s, keep_f, CHUNK - 1, CHUNK // 2 - 1)
        oacc_s[sl, :] = o
        return st

    def bwd(i, st):
        sl, o, st = chunk(n_chunks - 1 - i, st, gb_s, keep_b, 0, CHUNK // 2)
        oacc_s[sl, :] += o
        return st

    st_f = lax.fori_loop(0, n_chunks, fwd, s0f_ref[...].T)
    st_b = lax.fori_loop(0, n_chunks, bwd, s0b_ref[...].T)
    sf_ref[...] = st_f.T
    sb_ref[...] = st_b.T
    og = og_ref[...]
    o_ref[...] = (_rms(oacc_s[...], gn_ref[...], GLA_DV) * (og / (1.0 + jnp.exp(-og)))).astype(BF16)


def _gla(proj, wg, bg, s0f, s0b, gn, *, n_batch, t, row0):
    rb0 = row0 // t
    dk_blk, dv_blk = P_GQ // GLA_DK, P_GV // GLA_DV
    tok = lambda width, cb: pl.BlockSpec((t, width), lambda b, h, cb=cb: (rb0 + b, cb + h))
    state = pl.BlockSpec((None, None, GLA_DK, GLA_DV), lambda b, h: (b, h, 0, 0))
    return pl.pallas_call(
        functools.partial(_gla_kernel, n_chunks=t // CHUNK),
        out_shape=(jax.ShapeDtypeStruct((n_batch * t, GLA_OUT), BF16),
                   jax.ShapeDtypeStruct((n_batch, GLA_HEADS, GLA_DK, GLA_DV), F32),
                   jax.ShapeDtypeStruct((n_batch, GLA_HEADS, GLA_DK, GLA_DV), F32)),
        grid=(n_batch, GLA_HEADS),
        in_specs=[
            tok(GLA_DK, dk_blk),
            tok(GLA_DK, P_GK // GLA_DK),
            tok(GLA_DV, dv_blk),
            pl.BlockSpec((t, V7X_LANES), lambda b, h: (rb0 + b, P_GATE // V7X_LANES)),
            pl.BlockSpec((V7X_LANES, GLA_DK), lambda b, h: (0, h)),
            pl.BlockSpec((V7X_LANES, GLA_DK), lambda b, h: (0, GLA_HEADS + h)),
            pl.BlockSpec((1, GLA_DK), lambda b, h: (0, h)),
            pl.BlockSpec((1, GLA_DK), lambda b, h: (0, GLA_HEADS + h)),
            state, state,
            tok(GLA_DV, P_OG // GLA_DV),
            pl.BlockSpec((1, GLA_DV), lambda b, h: (0, 0)),
        ],
        out_specs=(pl.BlockSpec((t, GLA_DV), lambda b, h: (b, h)), state, state),
        scratch_shapes=[pltpu.VMEM((t, GLA_DK), F32), pltpu.VMEM((t, GLA_DK), F32),
                        pltpu.VMEM((t, GLA_DV), F32)],
        compiler_params=_cparams(("parallel", "parallel"), 24 * t * GLA_DV * 4),
        name="gla_%d" % t,
    )(proj, proj, proj, proj, wg, wg, bg, bg, s0f, s0b, proj, gn)


def _pack_w_in(w):
    z = lambda n: jnp.zeros((D_MODEL, n), w.dtype)
    return jnp.concatenate([
        w[:, :OFF_KV],
        w[:, OFF_KV:OFF_KV + KV_LORA],
        w[:, OFF_KV + KV_LORA:OFF_GQ], z(V7X_LANES - QK_ROPE),
        w[:, OFF_GATE:OFF_OG], z(V7X_LANES - 2 * GATE_RANK),
        z(P_GQ - P_GATE - V7X_LANES),
        w[:, OFF_GQ:OFF_GK], w[:, OFF_GK:OFF_GV], w[:, OFF_GV:OFF_GATE], w[:, OFF_OG:],
    ], axis=1).astype(BF16)


def _pack_w_qb(w):
    w = w.reshape(Q_LORA, MLA_HEADS, QK_NOPE + QK_ROPE)
    w = jnp.pad(w, ((0, 0), (0, 0), (0, QK_PAD - QK_NOPE - QK_ROPE)))
    return w.reshape(Q_LORA, MLA_HEADS * QK_PAD).astype(BF16)


def _pack_w_kvb(w):
    w = w.reshape(KV_LORA, MLA_HEADS, QK_NOPE + V_HEAD)
    return jnp.concatenate([w[:, :, :QK_NOPE].reshape(KV_LORA, MLA_OUT),
                            w[:, :, QK_NOPE:].reshape(KV_LORA, MLA_OUT)], axis=1).astype(BF16)


def _pack_gate(w_gf2, w_gb2, b_gf, b_gb):
    wg = jnp.zeros((V7X_LANES, 2 * GLA_QK), F32)
    wg = wg.at[:GATE_RANK, :GLA_QK].set(w_gf2).at[GATE_RANK:2 * GATE_RANK, GLA_QK:].set(w_gb2)
    return wg.astype(BF16), jnp.concatenate([b_gf, b_gb]).reshape(1, 2 * GLA_QK)


def _pad_lanes(g):
    return jnp.pad(g, (0, V7X_LANES - g.shape[0])).reshape(1, V7X_LANES)


def _rope_tables():
    rows = DEC_SEQ // GRID_W
    row = jnp.repeat(jnp.arange(rows, dtype=F32), GRID_W)
    col = jnp.tile(jnp.arange(GRID_W, dtype=F32), rows)
    n_freq = QK_ROPE // 4
    inv = jnp.power(ROPE_BASE, -jnp.arange(n_freq, dtype=F32) / n_freq)
    ang = jnp.concatenate([row[:, None] * inv, col[:, None] * inv], axis=-1)
    cos, sin = jnp.cos(ang), jnp.sin(ang)
    half = QK_ROPE // 2
    zpad = jnp.zeros((DEC_SEQ, V7X_LANES - QK_ROPE), F32)
    zhalf = jnp.zeros((DEC_SEQ, half), F32)
    cos_t = jnp.concatenate([cos, cos, zpad], axis=1)
    sa_t = jnp.concatenate([-sin, zhalf, zpad], axis=1)
    sb_t = jnp.concatenate([zhalf, sin, zpad], axis=1)
    ident = jnp.concatenate([jnp.ones((DEC_SEQ, QK_ROPE), F32), zpad], axis=1)
    zeros = jnp.zeros((DEC_SEQ, V7X_LANES), F32)
    return (jnp.stack([ident, cos_t]), jnp.stack([zeros, sa_t]), jnp.stack([zeros, sb_t]))


def kernel(x_prompt, x_sample, c, cache_ckv, cache_krope, state_gla_fwd, state_gla_bwd, c_ctx, w_ada, b_ada, norm1, norm2, w_in, q_a_norm, w_qb, kv_a_norm, w_kvb, q_norm_nope, q_norm_rope, k_norm_nope, k_norm_rope, w_gf2, b_gf, w_gb2, b_gb, gla_norm, mla_out_norm, w_o, w_up, w_down):
    x = jnp.concatenate([x_prompt.reshape(N_PROMPT, D_MODEL), x_sample.reshape(N_SAMPLE, D_MODEL)], axis=0)
    cvec = jnp.concatenate([c_ctx[None, :], c, jnp.zeros((N_MOD_ROWS - 1 - DEC_BATCH, D_MODEL), F32)], axis=0)
    mods = _ada(cvec, w_ada, b_ada).reshape(DEPTH, N_MOD_ROWS, 6, D_MODEL)
    rope_tabs = _rope_tables()
    zero_state = jnp.zeros((BATCH, GLA_HEADS, GLA_DK, GLA_DV), F32)

    ckvs, kropes, sfs, sbs = [], [], [], []
    for l in range(DEPTH):
        mod = mods[l]
        h = _normmod(x, norm1[l], mod, 0, 1)
        proj = _matmul(h, _pack_w_in(w_in[l]), tm=1024, tn=1024, tk=1024, out_dtype=F32, name="in_proj")

        q = _q_proj(proj, q_a_norm[l].reshape(1, Q_LORA), _pack_w_qb(w_qb[l]),
                    q_norm_nope[l].reshape(1, QK_NOPE), _pad_lanes(q_norm_rope[l]), rope_tabs)
        ckv, kr, krr = _kv_norm(proj, kv_a_norm[l].reshape(1, KV_LORA), _pad_lanes(k_norm_rope[l]), rope_tabs)
        ckv_all = jnp.concatenate([ckv, cache_ckv[:, l].reshape(N_CACHE, KV_LORA)], axis=0)
        kv = _kv_up(ckv_all, _pack_w_kvb(w_kvb[l]), k_norm_nope[l].reshape(1, QK_NOPE))
        kr_cache = jnp.pad(cache_krope[:, l].reshape(N_CACHE, QK_ROPE),
                           ((0, 0), (0, V7X_LANES - QK_ROPE))).astype(BF16)
        kr_all = jnp.concatenate([krr, kr_cache], axis=0)
        g_mla = mla_out_norm[l].reshape(1, MLA_OUT)
        o_mla_p = _attention(q, kv, kr_all, g_mla, n_batch=BATCH, t_q=SEQ, q_row0=0,
                             segs=[(SEQ, 0)])
        o_mla_s = _attention(q, kv, kr_all, g_mla, n_batch=DEC_BATCH, t_q=DEC_SEQ, q_row0=N_PROMPT,
                             segs=[(DEC_SEQ, N_PROMPT), (PAST_LEN, N_TOK)])

        wg, bg = _pack_gate(w_gf2[l], w_gb2[l], b_gf[l], b_gb[l])
        gn = gla_norm[l].reshape(1, GLA_DV)
        o_gla_p, s_f, s_b = _gla(proj, wg, bg, zero_state, zero_state, gn, n_batch=BATCH, t=SEQ, row0=0)
        o_gla_s, _, _ = _gla(proj, wg, bg, state_gla_fwd[:, l], state_gla_bwd[:, l], gn,
                             n_batch=DEC_BATCH, t=DEC_SEQ, row0=N_PROMPT)

        mix = jnp.concatenate([jnp.concatenate([o_mla_p, o_gla_p], axis=1),
                               jnp.concatenate([o_mla_s, o_gla_s], axis=1)], axis=0)
        x = _matmul_resid(mix, w_o[l], x, mod, 2, tm=1024, tn=1024, tk=1024, name="out_proj")

        h2 = _normmod(x, norm2[l], mod, 3, 4)
        act = _matmul(h2, w_up[l], tm=1024, tn=1024, tk=1024, out_dtype=BF16, relu2=True, name="mlp_up")
        x = _matmul_resid(act, w_down[l], x, mod, 5, tm=1024, tn=1024, tk=1024, name="mlp_down")

        ckvs.append(ckv[:N_PROMPT].reshape(BATCH, SEQ, KV_LORA))
        kropes.append(kr[:N_PROMPT, :QK_ROPE].reshape(BATCH, SEQ, QK_ROPE))
        sfs.append(s_f)
        sbs.append(s_b)

    return (x[:N_PROMPT].reshape(BATCH, SEQ, D_MODEL),
            x[N_PROMPT:].reshape(DEC_BATCH, DEC_SEQ, D_MODEL),
            jnp.stack(ckvs, axis=1), jnp.stack(kropes, axis=1),
            jnp.stack(sfs, axis=1), jnp.stack(sbs, axis=1))
```

```python
import functools

import jax
import jax.numpy as jnp
from jax import lax
from jax.experimental import pallas as pl
from jax.experimental.pallas import tpu as pltpu

D_MODEL = 4096
BATCH = 16
SEQ = 256
DEPTH = 2
DEC_BATCH = 4
DEC_SEQ = 1024
PAST_LEN = 512
GRID_W = 64
MLA_HEADS = 16
QK_NOPE = 128
QK_ROPE = 64
V_HEAD = 128
Q_LORA = 1024
KV_LORA = 512
GLA_HEADS = 8
GLA_DK = 128
GLA_DV = 256
GATE_RANK = 16
GATE_TAU = 16.0
CHUNK = 64
D_FF = 4 * D_MODEL
ROPE_BASE = 10000.0
EPS = 1e-6

MLA_OUT = MLA_HEADS * V_HEAD
GLA_QK = GLA_HEADS * GLA_DK
GLA_OUT = GLA_HEADS * GLA_DV
OFF_KV = Q_LORA
OFF_GQ = OFF_KV + KV_LORA + QK_ROPE
OFF_GK = OFF_GQ + GLA_QK
OFF_GV = OFF_GK + GLA_QK
OFF_GATE = OFF_GV + GLA_OUT
OFF_OG = OFF_GATE + 2 * GATE_RANK
N_IN = OFF_OG + GLA_OUT

F32 = jnp.float32
BF16 = jnp.bfloat16

V7X_LANES = 128
V7X_SCOPED_VMEM_BYTES = 60000 * 1024

N_PROMPT = BATCH * SEQ
N_SAMPLE = DEC_BATCH * DEC_SEQ
N_TOK = N_PROMPT + N_SAMPLE
N_CACHE = DEC_BATCH * PAST_LEN
N_MOD_ROWS = 8

P_KV = 1024
P_KROPE = P_KV + KV_LORA
P_GATE = P_KROPE + V7X_LANES
P_GQ = 2048
P_GK = P_GQ + GLA_QK
P_GV = P_GK + GLA_QK
P_OG = P_GV + GLA_OUT
P_TOTAL = P_OG + GLA_OUT
QK_PAD = 2 * V7X_LANES

ATTN_SCALE = float((QK_NOPE + QK_ROPE) ** -0.5)
GLA_SCALE = float(GLA_DK ** -0.5)


def _cparams(sem, vmem_bytes):
    return pltpu.CompilerParams(dimension_semantics=sem,
                                vmem_limit_bytes=min(int(vmem_bytes), V7X_SCOPED_VMEM_BYTES))


def _mod_row(row_block, rows_per_block):
    return jnp.maximum(row_block * rows_per_block // DEC_SEQ - (N_PROMPT // DEC_SEQ - 1), 0)


def _rms(x, g, n):
    ms = jnp.sum(x * x, axis=-1, keepdims=True) * (1.0 / n)
    return x * lax.rsqrt(ms + EPS) * g


def _ada_kernel(c_ref, w_ref, b_ref, o_ref):
    k = pl.program_id(2)

    @pl.when(k == 0)
    def _():
        o_ref[...] = jnp.broadcast_to(b_ref[...], o_ref.shape)

    c = c_ref[...]
    s = (c / (1.0 + jnp.exp(-c))).astype(BF16)
    o_ref[...] += jnp.dot(s, w_ref[...].astype(BF16), preferred_element_type=F32)


def _ada(cvec, w_ada, b_ada):
    tk, tn = 1024, 2048
    n_out = 6 * D_MODEL
    return pl.pallas_call(
        _ada_kernel,
        out_shape=jax.ShapeDtypeStruct((DEPTH, N_MOD_ROWS, n_out), F32),
        grid=(DEPTH, n_out // tn, D_MODEL // tk),
        in_specs=[
            pl.BlockSpec((N_MOD_ROWS, tk), lambda l, j, k: (0, k)),
            pl.BlockSpec((None, tk, tn), lambda l, j, k: (l, k, j)),
            pl.BlockSpec((None, 1, tn), lambda l, j, k: (l, 0, j)),
        ],
        out_specs=pl.BlockSpec((None, N_MOD_ROWS, tn), lambda l, j, k: (l, 0, j)),
        compiler_params=_cparams(("parallel", "parallel", "arbitrary"), 3 * tk * tn * 4),
        name="ada_mod",
    )(cvec, w_ada, b_ada.reshape(DEPTH, 1, n_out))


def _normmod_kernel(x_ref, g_ref, mod_ref, o_ref, *, shift_idx, scale_idx):
    y = _rms(x_ref[...], g_ref[...], D_MODEL)
    sc = mod_ref[0, scale_idx:scale_idx + 1, :]
    sh = mod_ref[0, shift_idx:shift_idx + 1, :]
    o_ref[...] = (y * (1.0 + sc) + sh).astype(BF16)


def _normmod(x, g, mod, shift_idx, scale_idx):
    tm = 256
    return pl.pallas_call(
        functools.partial(_normmod_kernel, shift_idx=shift_idx, scale_idx=scale_idx),
        out_shape=jax.ShapeDtypeStruct((N_TOK, D_MODEL), BF16),
        grid=(N_TOK // tm,),
        in_specs=[
            pl.BlockSpec((tm, D_MODEL), lambda i: (i, 0)),
            pl.BlockSpec((1, D_MODEL), lambda i: (0, 0)),
            pl.BlockSpec((1, 6, D_MODEL), lambda i: (_mod_row(i, tm), 0, 0)),
        ],
        out_specs=pl.BlockSpec((tm, D_MODEL), lambda i: (i, 0)),
        compiler_params=_cparams(("parallel",), 4 * tm * D_MODEL * 4),
        name="norm_mod",
    )(x, g.reshape(1, D_MODEL), mod)


def _resid_normmod_kernel(x_ref, f_ref, gmod_ref, g_ref, nmod_ref, xo_ref, h_ref, *, gate_idx, shift_idx, scale_idx):
    x = x_ref[...] + gmod_ref[0, gate_idx:gate_idx + 1, :] * f_ref[...]
    xo_ref[...] = x
    y = _rms(x, g_ref[...], D_MODEL)
    sc = nmod_ref[0, scale_idx:scale_idx + 1, :]
    sh = nmod_ref[0, shift_idx:shift_idx + 1, :]
    h_ref[...] = (y * (1.0 + sc) + sh).astype(BF16)


def _resid_normmod(x, f, gate_mod, gate_idx, g, norm_mod, shift_idx, scale_idx):
    tm = 256
    row = pl.BlockSpec((tm, D_MODEL), lambda i: (i, 0))
    mod = pl.BlockSpec((1, 6, D_MODEL), lambda i: (_mod_row(i, tm), 0, 0))
    return pl.pallas_call(
        functools.partial(_resid_normmod_kernel, gate_idx=gate_idx, shift_idx=shift_idx, scale_idx=scale_idx),
        out_shape=(jax.ShapeDtypeStruct((N_TOK, D_MODEL), F32), jax.ShapeDtypeStruct((N_TOK, D_MODEL), BF16)),
        grid=(N_TOK // tm,),
        in_specs=[row, row, mod, pl.BlockSpec((1, D_MODEL), lambda i: (0, 0)), mod],
        out_specs=(row, row),
        compiler_params=_cparams(("parallel",), 10 * tm * D_MODEL * 4),
        name="resid_norm_mod",
    )(x, f, gate_mod, g.reshape(1, D_MODEL), norm_mod)


def _resid_split_kernel(x_ref, f_ref, gmod_ref, op_ref, os_ref, *, gate_idx, prompt_blocks):
    x = x_ref[...] + gmod_ref[0, gate_idx:gate_idx + 1, :] * f_ref[...]
    i = pl.program_id(0)

    @pl.when(i < prompt_blocks)
    def _():
        op_ref[...] = x

    @pl.when(i >= prompt_blocks)
    def _():
        os_ref[...] = x


def _resid_split(x, f, gate_mod, gate_idx):
    tm = 256
    pb = N_PROMPT // tm
    row = pl.BlockSpec((tm, D_MODEL), lambda i: (i, 0))
    return pl.pallas_call(
        functools.partial(_resid_split_kernel, gate_idx=gate_idx, prompt_blocks=pb),
        out_shape=(jax.ShapeDtypeStruct((N_PROMPT, D_MODEL), F32), jax.ShapeDtypeStruct((N_SAMPLE, D_MODEL), F32)),
        grid=(N_TOK // tm,),
        in_specs=[row, row, pl.BlockSpec((1, 6, D_MODEL), lambda i: (_mod_row(i, tm), 0, 0))],
        out_specs=(pl.BlockSpec((tm, D_MODEL), lambda i: (jnp.minimum(i, pb - 1), 0)),
                   pl.BlockSpec((tm, D_MODEL), lambda i: (jnp.maximum(i - pb, 0), 0))),
        compiler_params=_cparams(("arbitrary",), 10 * tm * D_MODEL * 4),
        name="resid_out",
    )(x, f, gate_mod)


def _mm_kernel(a_ref, b_ref, o_ref, *scratch, nk, relu2):
    acc_ref = scratch[0] if scratch else o_ref
    k = pl.program_id(2)

    def prod():
        return jnp.dot(a_ref[...], b_ref[...].astype(BF16), preferred_element_type=F32)

    def finish(r):
        if relu2:
            r = jnp.square(jnp.maximum(r, 0.0))
        o_ref[...] = r.astype(o_ref.dtype)

    if nk == 1:
        finish(prod())
        return

    @pl.when(k == 0)
    def _():
        acc_ref[...] = prod()

    @pl.when(jnp.logical_and(k > 0, k < nk - 1))
    def _():
        acc_ref[...] += prod()

    @pl.when(k == nk - 1)
    def _():
        finish(acc_ref[...] + prod())


def _matmul(a, b, layer, *, out_dtype, relu2=False, name):
    tm, tn, tk = 2048, 1024, 1024
    m, kdim = a.shape
    n = b.shape[-1]
    nk = kdim // tk
    if layer is None:
        b_spec = pl.BlockSpec((tk, tn), lambda i, j, k: (k, j))
    else:
        b_spec = pl.BlockSpec((None, tk, tn), lambda i, j, k: (layer, k, j))
    o_bytes = jnp.dtype(out_dtype).itemsize
    needs_acc = jnp.dtype(out_dtype) != jnp.dtype(F32)
    vmem = (2 * (tm * tk * 2 + tk * tn * b.dtype.itemsize + tm * tn * o_bytes)
            + tk * tn * 2 + (2 + needs_acc) * tm * tn * 4)
    return pl.pallas_call(
        functools.partial(_mm_kernel, nk=nk, relu2=relu2),
        out_shape=jax.ShapeDtypeStruct((m, n), out_dtype),
        grid=(m // tm, n // tn, nk),
        in_specs=[pl.BlockSpec((tm, tk), lambda i, j, k: (i, k)), b_spec],
        out_specs=pl.BlockSpec((tm, tn), lambda i, j, k: (i, j)),
        scratch_shapes=[pltpu.VMEM((tm, tn), F32)] if needs_acc else [],
        compiler_params=_cparams(("parallel", "parallel", "arbitrary"), vmem),
        name=name,
    )(a, b)


def _rope128(r, cos, sa, sb):
    return r * cos + pltpu.roll(r, 96, 1) * sa + pltpu.roll(r, 32, 1) * sb


def _q_kernel(ql_ref, ga_ref, w_ref, gn_ref, gr_ref, cos_ref, sa_ref, sb_ref, o_ref, qn_ref, *, heads_per_tile):
    @pl.when(pl.program_id(1) == 0)
    def _():
        qn_ref[...] = _rms(ql_ref[...], ga_ref[...], Q_LORA).astype(BF16)

    acc = jnp.dot(qn_ref[...], w_ref[...].astype(BF16), preferred_element_type=F32)
    cos, sa, sb = cos_ref[...], sa_ref[...], sb_ref[...]
    for hh in range(heads_per_tile):
        c0 = hh * QK_PAD
        nope = acc[:, c0:c0 + QK_NOPE]
        rope = acc[:, c0 + QK_NOPE:c0 + QK_PAD]
        nope = _rms(nope, gn_ref[...], QK_NOPE) * ATTN_SCALE
        rope = _rope128(_rms(rope, gr_ref[...], QK_ROPE), cos, sa, sb) * ATTN_SCALE
        o_ref[:, c0:c0 + QK_NOPE] = nope.astype(BF16)
        o_ref[:, c0 + QK_NOPE:c0 + QK_PAD] = rope.astype(BF16)


def _rope_specs(tm, nargs):
    blocks_per_seq = DEC_SEQ // tm

    def idx(i, *_):
        return (jnp.minimum(i * tm // N_PROMPT, 1), i % blocks_per_seq, 0)

    return [pl.BlockSpec((None, tm, V7X_LANES), idx) for _ in range(nargs)]


def _q_proj(proj, ga, w_qb_p, gn, gr128, rope_tabs):
    tm, tn = 512, 1024
    n = MLA_HEADS * QK_PAD
    return pl.pallas_call(
        functools.partial(_q_kernel, heads_per_tile=tn // QK_PAD),
        out_shape=jax.ShapeDtypeStruct((N_TOK, n), BF16),
        grid=(N_TOK // tm, n // tn),
        in_specs=[
            pl.BlockSpec((tm, Q_LORA), lambda i, j: (i, 0)),
            pl.BlockSpec((1, Q_LORA), lambda i, j: (0, 0)),
            pl.BlockSpec((Q_LORA, tn), lambda i, j: (0, j)),
            pl.BlockSpec((1, QK_NOPE), lambda i, j: (0, 0)),
            pl.BlockSpec((1, V7X_LANES), lambda i, j: (0, 0)),
        ] + _rope_specs(tm, 3),
        out_specs=pl.BlockSpec((tm, tn), lambda i, j: (i, j)),
        scratch_shapes=[pltpu.VMEM((tm, Q_LORA), BF16)],
        compiler_params=_cparams(
            ("parallel", "arbitrary"),
            2 * (tm * Q_LORA * 4 + Q_LORA * tn * 2 + tm * tn * 2) + tm * Q_LORA * 2 + 3 * tm * tn * 4),
        name="mla_q_proj",
    )(proj, ga, w_qb_p, gn, gr128, *rope_tabs)


def _kvnorm_kernel(c_ref, r_ref, gc_ref, gk_ref, cos_ref, sa_ref, sb_ref, ckv_ref, kr_ref, krr_ref):
    ckv_ref[...] = _rms(c_ref[...], gc_ref[...], KV_LORA)
    rn = _rms(r_ref[...], gk_ref[...], QK_ROPE)
    kr_ref[...] = rn
    krr_ref[...] = _rope128(rn, cos_ref[...], sa_ref[...], sb_ref[...]).astype(BF16)


def _kv_norm(proj, gc, gk128, rope_tabs):
    tm = 512
    return pl.pallas_call(
        _kvnorm_kernel,
        out_shape=(jax.ShapeDtypeStruct((N_TOK, KV_LORA), F32),
                   jax.ShapeDtypeStruct((N_TOK, V7X_LANES), F32),
                   jax.ShapeDtypeStruct((N_TOK, V7X_LANES), BF16)),
        grid=(N_TOK // tm,),
        in_specs=[
            pl.BlockSpec((tm, KV_LORA), lambda i: (i, P_KV // KV_LORA)),
            pl.BlockSpec((tm, V7X_LANES), lambda i: (i, P_KROPE // V7X_LANES)),
            pl.BlockSpec((1, KV_LORA), lambda i: (0, 0)),
            pl.BlockSpec((1, V7X_LANES), lambda i: (0, 0)),
        ] + _rope_specs(tm, 3),
        out_specs=(pl.BlockSpec((tm, KV_LORA), lambda i: (i, 0)),
                   pl.BlockSpec((tm, V7X_LANES), lambda i: (i, 0)),
                   pl.BlockSpec((tm, V7X_LANES), lambda i: (i, 0))),
        compiler_params=_cparams(("parallel",), 8 * tm * (KV_LORA + 4 * V7X_LANES) * 4),
        name="mla_kv_norm",
    )(proj, proj, gc, gk128, *rope_tabs)


def _kvup_kernel(c_ref, w_ref, gn_ref, o_ref, *, heads_per_tile, k_tiles):
    acc = jnp.dot(c_ref[...].astype(BF16), w_ref[...].astype(BF16), preferred_element_type=F32)

    @pl.when(pl.program_id(1) < k_tiles)
    def _():
        for hh in range(heads_per_tile):
            c0 = hh * QK_NOPE
            o_ref[:, c0:c0 + QK_NOPE] = _rms(acc[:, c0:c0 + QK_NOPE], gn_ref[...], QK_NOPE).astype(BF16)

    @pl.when(pl.program_id(1) >= k_tiles)
    def _():
        o_ref[...] = acc.astype(BF16)


def _kv_up(ckv_all, w_kvb_p, gn):
    tm, tn = 512, 1024
    m = ckv_all.shape[0]
    n = 2 * MLA_OUT
    return pl.pallas_call(
        functools.partial(_kvup_kernel, heads_per_tile=tn // QK_NOPE, k_tiles=MLA_OUT // tn),
        out_shape=jax.ShapeDtypeStruct((m, n), BF16),
        grid=(m // tm, n // tn),
        in_specs=[
            pl.BlockSpec((tm, KV_LORA), lambda i, j: (i, 0)),
            pl.BlockSpec((KV_LORA, tn), lambda i, j: (0, j)),
            pl.BlockSpec((1, QK_NOPE), lambda i, j: (0, 0)),
        ],
        out_specs=pl.BlockSpec((tm, tn), lambda i, j: (i, j)),
        compiler_params=_cparams(
            ("parallel", "parallel"),
            2 * (tm * KV_LORA * 4 + KV_LORA * tn * 2 + tm * tn * 2) + 3 * tm * tn * 4),
        name="mla_kv_up",
    )(ckv_all, w_kvb_p, gn)


def _attn_kernel(*refs, n_seg):
    q_ref = refs[0]
    seg_refs = [refs[1 + 3 * s:4 + 3 * s] for s in range(n_seg)]
    g_ref = refs[1 + 3 * n_seg]
    o_ref = refs[2 + 3 * n_seg]
    oacc_ref = refs[3 + 3 * n_seg]
    for h in range(MLA_HEADS):
        qh = q_ref[:, h * QK_PAD:(h + 1) * QK_PAD]
        scores = []
        for kn_ref, kr_ref, _ in seg_refs:
            kh = jnp.concatenate([kn_ref[:, h * QK_NOPE:(h + 1) * QK_NOPE], kr_ref[...]], axis=1)
            scores.append(lax.dot_general(qh, kh, (((1,), (1,)), ((), ())), preferred_element_type=F32))
        m = scores[0].max(axis=-1, keepdims=True)
        for s in scores[1:]:
            m = jnp.maximum(m, s.max(axis=-1, keepdims=True))
        denom = None
        out = None
        for s, (_, _, v_ref) in zip(scores, seg_refs):
            p = jnp.exp(s - m)
            ps = p.sum(axis=-1, keepdims=True)
            pv = jnp.dot(p.astype(BF16), v_ref[:, h * V_HEAD:(h + 1) * V_HEAD], preferred_element_type=F32)
            denom = ps if denom is None else denom + ps
            out = pv if out is None else out + pv
        oacc_ref[:, h * V_HEAD:(h + 1) * V_HEAD] = out / denom
    o_ref[...] = _rms(oacc_ref[...], g_ref[...], MLA_OUT).astype(BF16)


def _attention(q, kv, kr, g_out, *, n_batch, t_q, segs, q_row0):
    tq = 256
    qb = t_q // tq
    in_specs = [pl.BlockSpec((tq, MLA_HEADS * QK_PAD), lambda b, i: (q_row0 // tq + b * qb + i, 0))]
    args = [q]
    vmem = 2 * tq * MLA_HEADS * QK_PAD * 2
    for rows, row0 in segs:
        blk0 = row0 // rows
        in_specs += [
            pl.BlockSpec((rows, MLA_OUT), lambda b, i, blk0=blk0: (blk0 + b, 0)),
            pl.BlockSpec((rows, V7X_LANES), lambda b, i, blk0=blk0: (blk0 + b, 0)),
            pl.BlockSpec((rows, MLA_OUT), lambda b, i, blk0=blk0: (blk0 + b, 1)),
        ]
        args += [kv, kr, kv]
        vmem += 2 * rows * (2 * MLA_OUT + V7X_LANES) * 2 + 12 * tq * rows * 4
    in_specs.append(pl.BlockSpec((1, MLA_OUT), lambda b, i: (0, 0)))
    args.append(g_out)
    vmem += 4 * tq * MLA_OUT * 4
    return pl.pallas_call(
        functools.partial(_attn_kernel, n_seg=len(segs)),
        out_shape=jax.ShapeDtypeStruct((n_batch * t_q, MLA_OUT), BF16),
        grid=(n_batch, qb),
        in_specs=in_specs,
        out_specs=pl.BlockSpec((tq, MLA_OUT), lambda b, i: (b * qb + i, 0)),
        scratch_shapes=[pltpu.VMEM((tq, MLA_OUT), F32)],
        compiler_params=_cparams(("parallel", "arbitrary"), vmem),
        name="mla_attention_%d" % t_q,
    )(*args)


def _log_sigmoid(x):
    return -(jnp.maximum(-x, 0.0) + jnp.log1p(jnp.exp(-jnp.abs(x))))


GLA_HEAD_GROUP = 4


def _gla_kernel(q_ref, k_ref, v_ref, glr_ref, wgf_ref, wgb_ref, bgf_ref, bgb_ref, s0f_ref, s0b_ref,
                og_ref, gn_ref, o_ref, sf_ref, sb_ref, gf_s, gb_s, of_s, ob_s, st_s, *, n_chunks):
    hg = GLA_HEAD_GROUP
    glr = glr_ref[...].astype(BF16)

    def gate(w_ref, b_ref):
        x = jnp.dot(glr, w_ref[...].astype(BF16), preferred_element_type=F32) + b_ref[...]
        return _log_sigmoid(x) * (1.0 / GATE_TAU)

    gf_s[...] = gate(wgf_ref, bgf_ref)
    gb_s[...] = gate(wgb_ref, bgb_ref)
    for j in range(hg):
        st_s[j] = s0f_ref[j].T
        st_s[hg + j] = s0b_ref[j].T

    row = lax.broadcasted_iota(jnp.int32, (CHUNK, CHUNK), 0)
    col = lax.broadcasted_iota(jnp.int32, (CHUNK, CHUNK), 1)
    nt = (((1,), (1,)), ((), ()))
    tn = (((0,), (0,)), ((), ()))

    keep_f = col <= row
    keep_b = col >= row

    def cumulate(n, carry):
        sl = pl.ds(pl.multiple_of(n * CHUNK, CHUNK), CHUNK)
        gf_s[sl, :] = jnp.dot(jnp.where(keep_f, 1.0, 0.0), gf_s[sl, :],
                              precision=lax.Precision.HIGHEST, preferred_element_type=F32)
        gb_s[sl, :] = jnp.dot(jnp.where(keep_b, 1.0, 0.0), gb_s[sl, :],
                              precision=lax.Precision.HIGHEST, preferred_element_type=F32)
        return carry

    lax.fori_loop(0, n_chunks, cumulate, 0)

    def step(n, carry):
        units = []
        for j in range(hg):
            units.append((n, j, j, gf_s, of_s, keep_f, CHUNK - 1, CHUNK // 2 - 1))
            units.append((n_chunks - 1 - n, j, hg + j, gb_s, ob_s, keep_b, 0, CHUNK // 2))
        staged = []
        for cn, j, slot, b_s, o_s, keep, end_idx, mid_idx in units:
            sl = pl.ds(pl.multiple_of(cn * CHUNK, CHUNK), CHUNK)
            dk = slice(j * GLA_DK, (j + 1) * GLA_DK)
            dv = slice(j * GLA_DV, (j + 1) * GLA_DV)
            b = b_s[sl, dk]
            q = q_ref[sl, dk] * GLA_SCALE
            k = k_ref[sl, dk]
            v = v_ref[sl, dv].astype(BF16)
            b_end = b[end_idx:end_idx + 1, :]
            b_mid = b[mid_idx:mid_idx + 1, :]
            qe = (q * jnp.exp(b)).astype(BF16)
            qa = (q * jnp.exp(b - b_mid)).astype(BF16)
            ka = (k * jnp.exp(b_mid - b)).astype(BF16)
            kd = (k * jnp.exp(b_end - b)).astype(BF16)
            staged.append((sl, dv, slot, o_s, keep, v, qe, qa, ka, kd, jnp.exp(b_end)))
        scores = [lax.dot_general(qa, ka, nt, preferred_element_type=F32)
                  for _, _, _, _, _, _, _, qa, ka, _, _ in staged]
        inter = [lax.dot_general(qe, st_s[slot].astype(BF16), nt, preferred_element_type=F32)
                 for _, _, slot, _, _, _, qe, _, _, _, _ in staged]
        update = [lax.dot_general(v, kd, tn, preferred_element_type=F32)
                  for _, _, _, _, _, v, _, _, _, kd, _ in staged]
        for (sl, dv, slot, o_s, keep, v, _, _, _, _, decay), a, o_inter, upd in zip(staged, scores, inter, update):
            a = jnp.where(keep, a, 0.0).astype(BF16)
            o_s[sl, dv] = jnp.dot(a, v, preferred_element_type=F32) + o_inter
            st_s[slot] = st_s[slot] * decay + upd
        return carry

    lax.fori_loop(0, n_chunks, step, 0)
    for j in range(hg):
        sf_ref[j] = st_s[j].T
        sb_ref[j] = st_s[hg + j].T
        dv = slice(j * GLA_DV, (j + 1) * GLA_DV)
        og = og_ref[:, dv]
        o = _rms(of_s[:, dv] + ob_s[:, dv], gn_ref[...], GLA_DV)
        o_ref[:, dv] = (o * (og / (1.0 + jnp.exp(-og)))).astype(BF16)


def _gla(proj, wg, bg, s0f, s0b, gn, *, n_batch, t, row0):
    hg = GLA_HEAD_GROUP
    rb0 = row0 // t
    n_groups = GLA_HEADS // hg
    tok = lambda width, col0: pl.BlockSpec(
        (t, hg * width), lambda b, h, cb=col0 // (hg * width): (rb0 + b, cb + h))
    state = pl.BlockSpec((None, hg, GLA_DK, GLA_DV), lambda b, h: (b, h, 0, 0))
    return pl.pallas_call(
        functools.partial(_gla_kernel, n_chunks=t // CHUNK),
        out_shape=(jax.ShapeDtypeStruct((n_batch * t, GLA_OUT), BF16),
                   jax.ShapeDtypeStruct((n_batch, GLA_HEADS, GLA_DK, GLA_DV), F32),
                   jax.ShapeDtypeStruct((n_batch, GLA_HEADS, GLA_DK, GLA_DV), F32)),
        grid=(n_batch, n_groups),
        in_specs=[
            tok(GLA_DK, P_GQ),
            tok(GLA_DK, P_GK),
            tok(GLA_DV, P_GV),
            pl.BlockSpec((t, V7X_LANES), lambda b, h: (rb0 + b, P_GATE // V7X_LANES)),
            pl.BlockSpec((V7X_LANES, hg * GLA_DK), lambda b, h: (0, h)),
            pl.BlockSpec((V7X_LANES, hg * GLA_DK), lambda b, h: (0, n_groups + h)),
            pl.BlockSpec((1, hg * GLA_DK), lambda b, h: (0, h)),
            pl.BlockSpec((1, hg * GLA_DK), lambda b, h: (0, n_groups + h)),
            state, state,
            tok(GLA_DV, P_OG),
            pl.BlockSpec((1, GLA_DV), lambda b, h: (0, 0)),
        ],
        out_specs=(pl.BlockSpec((t, hg * GLA_DV), lambda b, h: (b, h)), state, state),
        scratch_shapes=[pltpu.VMEM((t, hg * GLA_DK), F32), pltpu.VMEM((t, hg * GLA_DK), F32),
                        pltpu.VMEM((t, hg * GLA_DV), F32), pltpu.VMEM((t, hg * GLA_DV), F32),
                        pltpu.VMEM((2 * hg, GLA_DV, GLA_DK), F32)],
        compiler_params=_cparams(
            ("parallel", "parallel"),
            2 * (t * hg * (2 * GLA_DK + 2 * GLA_DV) * 4 + t * V7X_LANES * 4 + t * hg * GLA_DV * 2)
            + t * hg * (2 * GLA_DK + 2 * GLA_DV) * 4 + 6 * t * hg * GLA_DK * 4
            + 10 * hg * GLA_DK * GLA_DV * 4 + 4 * V7X_LANES * hg * GLA_DK * 2),
        name="gla_%d" % t,
    )(proj, proj, proj, proj, wg, wg, bg, bg, s0f, s0b, proj, gn)


def _pack_w_in(w):
    z = lambda n: jnp.zeros((D_MODEL, n), w.dtype)
    return jnp.concatenate([
        w[:, :OFF_KV],
        w[:, OFF_KV:OFF_KV + KV_LORA],
        w[:, OFF_KV + KV_LORA:OFF_GQ], z(V7X_LANES - QK_ROPE),
        w[:, OFF_GATE:OFF_OG], z(V7X_LANES - 2 * GATE_RANK),
        z(P_GQ - P_GATE - V7X_LANES),
        w[:, OFF_GQ:OFF_GK], w[:, OFF_GK:OFF_GV], w[:, OFF_GV:OFF_GATE], w[:, OFF_OG:],
    ], axis=1).astype(BF16)


def _pack_w_qb(w):
    w = w.reshape(Q_LORA, MLA_HEADS, QK_NOPE + QK_ROPE)
    w = jnp.pad(w, ((0, 0), (0, 0), (0, QK_PAD - QK_NOPE - QK_ROPE)))
    return w.reshape(Q_LORA, MLA_HEADS * QK_PAD).astype(BF16)


def _pack_w_kvb(w):
    w = w.reshape(KV_LORA, MLA_HEADS, QK_NOPE + V_HEAD)
    return jnp.concatenate([w[:, :, :QK_NOPE].reshape(KV_LORA, MLA_OUT),
                            w[:, :, QK_NOPE:].reshape(KV_LORA, MLA_OUT)], axis=1).astype(BF16)


def _pack_gate(w_gf2, w_gb2, b_gf, b_gb):
    wg = jnp.zeros((V7X_LANES, 2 * GLA_QK), F32)
    wg = wg.at[:GATE_RANK, :GLA_QK].set(w_gf2).at[GATE_RANK:2 * GATE_RANK, GLA_QK:].set(w_gb2)
    return wg.astype(BF16), jnp.concatenate([b_gf, b_gb]).reshape(1, 2 * GLA_QK)


def _pad_lanes(g):
    return jnp.pad(g, (0, V7X_LANES - g.shape[0])).reshape(1, V7X_LANES)


def _rope_tables():
    rows = DEC_SEQ // GRID_W
    row = jnp.repeat(jnp.arange(rows, dtype=F32), GRID_W)
    col = jnp.tile(jnp.arange(GRID_W, dtype=F32), rows)
    n_freq = QK_ROPE // 4
    inv = jnp.power(ROPE_BASE, -jnp.arange(n_freq, dtype=F32) / n_freq)
    ang = jnp.concatenate([row[:, None] * inv, col[:, None] * inv], axis=-1)
    cos, sin = jnp.cos(ang), jnp.sin(ang)
    half = QK_ROPE // 2
    zpad = jnp.zeros((DEC_SEQ, V7X_LANES - QK_ROPE), F32)
    zhalf = jnp.zeros((DEC_SEQ, half), F32)
    cos_t = jnp.concatenate([cos, cos, zpad], axis=1)
    sa_t = jnp.concatenate([-sin, zhalf, zpad], axis=1)
    sb_t = jnp.concatenate([zhalf, sin, zpad], axis=1)
    ident = jnp.concatenate([jnp.ones((DEC_SEQ, QK_ROPE), F32), zpad], axis=1)
    zeros = jnp.zeros((DEC_SEQ, V7X_LANES), F32)
    return (jnp.stack([ident, cos_t]), jnp.stack([zeros, sa_t]), jnp.stack([zeros, sb_t]))


def kernel(x_prompt, x_sample, c, cache_ckv, cache_krope, state_gla_fwd, state_gla_bwd, c_ctx, w_ada, b_ada, norm1, norm2, w_in, q_a_norm, w_qb, kv_a_norm, w_kvb, q_norm_nope, q_norm_rope, k_norm_nope, k_norm_rope, w_gf2, b_gf, w_gb2, b_gb, gla_norm, mla_out_norm, w_o, w_up, w_down):
    x = jnp.concatenate([x_prompt.reshape(N_PROMPT, D_MODEL), x_sample.reshape(N_SAMPLE, D_MODEL)], axis=0)
    cvec = jnp.concatenate([c_ctx[None, :], c, jnp.zeros((N_MOD_ROWS - 1 - DEC_BATCH, D_MODEL), F32)], axis=0)
    mods = _ada(cvec, w_ada, b_ada).reshape(DEPTH, N_MOD_ROWS, 6, D_MODEL)
    rope_tabs = _rope_tables()
    zero_state = jnp.zeros((BATCH, GLA_HEADS, GLA_DK, GLA_DV), F32)

    ckvs, kropes, sfs, sbs = [], [], [], []
    h = _normmod(x, norm1[0], mods[0], 0, 1)
    for l in range(DEPTH):
        mod = mods[l]
        proj = _matmul(h, _pack_w_in(w_in[l]), None, out_dtype=F32, name="in_proj")

        q = _q_proj(proj, q_a_norm[l].reshape(1, Q_LORA), _pack_w_qb(w_qb[l]),
                    q_norm_nope[l].reshape(1, QK_NOPE), _pad_lanes(q_norm_rope[l]), rope_tabs)
        ckv, kr, krr = _kv_norm(proj, kv_a_norm[l].reshape(1, KV_LORA), _pad_lanes(k_norm_rope[l]), rope_tabs)
        ckv_all = jnp.concatenate([ckv, cache_ckv[:, l].reshape(N_CACHE, KV_LORA)], axis=0)
        kv = _kv_up(ckv_all, _pack_w_kvb(w_kvb[l]), k_norm_nope[l].reshape(1, QK_NOPE))
        kr_cache = jnp.pad(cache_krope[:, l].reshape(N_CACHE, QK_ROPE),
                           ((0, 0), (0, V7X_LANES - QK_ROPE))).astype(BF16)
        kr_all = jnp.concatenate([krr, kr_cache], axis=0)
        g_mla = mla_out_norm[l].reshape(1, MLA_OUT)
        o_mla_p = _attention(q, kv, kr_all, g_mla, n_batch=BATCH, t_q=SEQ, q_row0=0,
                             segs=[(SEQ, 0)])
        o_mla_s = _attention(q, kv, kr_all, g_mla, n_batch=DEC_BATCH, t_q=DEC_SEQ, q_row0=N_PROMPT,
                             segs=[(DEC_SEQ, N_PROMPT), (PAST_LEN, N_TOK)])

        wg, bg = _pack_gate(w_gf2[l], w_gb2[l], b_gf[l], b_gb[l])
        gn = gla_norm[l].reshape(1, GLA_DV)
        o_gla_p, s_f, s_b = _gla(proj, wg, bg, zero_state, zero_state, gn, n_batch=BATCH, t=SEQ, row0=0)
        o_gla_s, _, _ = _gla(proj, wg, bg, state_gla_fwd[:, l], state_gla_bwd[:, l], gn,
                             n_batch=DEC_BATCH, t=DEC_SEQ, row0=N_PROMPT)

        mix = jnp.concatenate([jnp.concatenate([o_mla_p, o_gla_p], axis=1),
                               jnp.concatenate([o_mla_s, o_gla_s], axis=1)], axis=0)
        f = _matmul(mix, w_o, l, out_dtype=F32, name="out_proj")
        x, h2 = _resid_normmod(x, f, mod, 2, norm2[l], mod, 3, 4)

        act = _matmul(h2, w_up, l, out_dtype=BF16, relu2=True, name="mlp_up")
        f = _matmul(act, w_down, l, out_dtype=F32, name="mlp_down")
        if l + 1 < DEPTH:
            x, h = _resid_normmod(x, f, mod, 5, norm1[l + 1], mods[l + 1], 0, 1)
        else:
            y_prompt, y_sample = _resid_split(x, f, mod, 5)

        ckvs.append(ckv[:N_PROMPT].reshape(BATCH, SEQ, KV_LORA))
        kropes.append(kr[:N_PROMPT, :QK_ROPE].reshape(BATCH, SEQ, QK_ROPE))
        sfs.append(s_f)
        sbs.append(s_b)

    return (y_prompt.reshape(BATCH, SEQ, D_MODEL),
            y_sample.reshape(DEC_BATCH, DEC_SEQ, D_MODEL),
            jnp.stack(ckvs, axis=1), jnp.stack(kropes, axis=1),
            jnp.stack(sfs, axis=1), jnp.stack(sbs, axis=1))
```

```python
import functools

import jax
import jax.numpy as jnp
from jax import lax
from jax.experimental import pallas as pl
from jax.experimental.pallas import tpu as pltpu

D_MODEL = 4096
BATCH = 16
SEQ = 256
DEPTH = 2
DEC_BATCH = 4
DEC_SEQ = 1024
PAST_LEN = 512
GRID_W = 64
MLA_HEADS = 16
QK_NOPE = 128
QK_ROPE = 64
V_HEAD = 128
Q_LORA = 1024
KV_LORA = 512
GLA_HEADS = 8
GLA_DK = 128
GLA_DV = 256
GATE_RANK = 16
GATE_TAU = 16.0
CHUNK = 64
D_FF = 4 * D_MODEL
ROPE_BASE = 10000.0
EPS = 1e-6

MLA_OUT = MLA_HEADS * V_HEAD
GLA_QK = GLA_HEADS * GLA_DK
GLA_OUT = GLA_HEADS * GLA_DV
OFF_KV = Q_LORA
OFF_GQ = OFF_KV + KV_LORA + QK_ROPE
OFF_GK = OFF_GQ + GLA_QK
OFF_GV = OFF_GK + GLA_QK
OFF_GATE = OFF_GV + GLA_OUT
OFF_OG = OFF_GATE + 2 * GATE_RANK
N_IN = OFF_OG + GLA_OUT

F32 = jnp.float32
BF16 = jnp.bfloat16

V7X_LANES = 128
V7X_SCOPED_VMEM_BYTES = 60000 * 1024

N_PROMPT = BATCH * SEQ
N_SAMPLE = DEC_BATCH * DEC_SEQ
N_TOK = N_PROMPT + N_SAMPLE
N_CACHE = DEC_BATCH * PAST_LEN
N_MOD_ROWS = 8

P_KV = 1024
P_KROPE = P_KV + KV_LORA
P_GATE = P_KROPE + V7X_LANES
P_GQ = 2048
P_GK = P_GQ + GLA_QK
P_GV = P_GK + GLA_QK
P_OG = P_GV + GLA_OUT
P_TOTAL = P_OG + GLA_OUT
QK_PAD = 2 * V7X_LANES

ATTN_SCALE = float((QK_NOPE + QK_ROPE) ** -0.5)
GLA_SCALE = float(GLA_DK ** -0.5)


def _cparams(sem, vmem_bytes):
    return pltpu.CompilerParams(dimension_semantics=sem,
                                vmem_limit_bytes=min(int(vmem_bytes), V7X_SCOPED_VMEM_BYTES))


def _mod_row(row_block, rows_per_block):
    return jnp.maximum(row_block * rows_per_block // DEC_SEQ - (N_PROMPT // DEC_SEQ - 1), 0)


def _rms(x, g, n):
    ms = jnp.sum(x * x, axis=-1, keepdims=True) * (1.0 / n)
    return x * lax.rsqrt(ms + EPS) * g


def _ada_kernel(c_ref, w_ref, b_ref, o_ref):
    k = pl.program_id(2)

    @pl.when(k == 0)
    def _():
        o_ref[...] = jnp.broadcast_to(b_ref[...], o_ref.shape)

    c = c_ref[...]
    s = (c / (1.0 + jnp.exp(-c))).astype(BF16)
    o_ref[...] += jnp.dot(s, w_ref[...].astype(BF16), preferred_element_type=F32)


def _ada(cvec, w_ada, b_ada):
    tk, tn = 1024, 2048
    n_out = 6 * D_MODEL
    return pl.pallas_call(
        _ada_kernel,
        out_shape=jax.ShapeDtypeStruct((DEPTH, N_MOD_ROWS, n_out), F32),
        grid=(DEPTH, n_out // tn, D_MODEL // tk),
        in_specs=[
            pl.BlockSpec((N_MOD_ROWS, tk), lambda l, j, k: (0, k)),
            pl.BlockSpec((None, tk, tn), lambda l, j, k: (l, k, j)),
            pl.BlockSpec((None, 1, tn), lambda l, j, k: (l, 0, j)),
        ],
        out_specs=pl.BlockSpec((None, N_MOD_ROWS, tn), lambda l, j, k: (l, 0, j)),
        compiler_params=_cparams(("parallel", "parallel", "arbitrary"), 3 * tk * tn * 4),
        name="ada_mod",
    )(cvec, w_ada, b_ada.reshape(DEPTH, 1, n_out))


ROW_TILE = 256


def _stream_specs(x):
    if not isinstance(x, tuple):
        return [pl.BlockSpec((ROW_TILE, D_MODEL), lambda i: (i, 0))], [x]
    pb = N_PROMPT // ROW_TILE
    return ([pl.BlockSpec((ROW_TILE, D_MODEL), lambda i: (jnp.minimum(i, pb - 1), 0)),
             pl.BlockSpec((ROW_TILE, D_MODEL), lambda i: (jnp.maximum(i - pb, 0), 0))], list(x))


def _load_stream(x_refs):
    if len(x_refs) == 1:
        return x_refs[0][...]
    in_prompt = pl.program_id(0) < N_PROMPT // ROW_TILE
    return jnp.where(in_prompt, x_refs[0][...], x_refs[1][...])


def _modulated_norm(x, g_ref, mod_ref, shift_idx, scale_idx):
    y = _rms(x, g_ref[...], D_MODEL)
    sc = mod_ref[0, scale_idx:scale_idx + 1, :]
    sh = mod_ref[0, shift_idx:shift_idx + 1, :]
    return (y * (1.0 + sc) + sh).astype(BF16)


def _normmod_kernel(*refs, n_x, shift_idx, scale_idx):
    g_ref, mod_ref, o_ref = refs[n_x:]
    o_ref[...] = _modulated_norm(_load_stream(refs[:n_x]), g_ref, mod_ref, shift_idx, scale_idx)


def _normmod(x, g, mod, shift_idx, scale_idx):
    x_specs, x_args = _stream_specs(x)
    row = pl.BlockSpec((ROW_TILE, D_MODEL), lambda i: (i, 0))
    return pl.pallas_call(
        functools.partial(_normmod_kernel, n_x=len(x_args), shift_idx=shift_idx, scale_idx=scale_idx),
        out_shape=jax.ShapeDtypeStruct((N_TOK, D_MODEL), BF16),
        grid=(N_TOK // ROW_TILE,),
        in_specs=x_specs + [
            pl.BlockSpec((1, D_MODEL), lambda i: (0, 0)),
            pl.BlockSpec((1, 6, D_MODEL), lambda i: (_mod_row(i, ROW_TILE), 0, 0)),
        ],
        out_specs=row,
        compiler_params=_cparams(("arbitrary",), 8 * ROW_TILE * D_MODEL * 4),
        name="norm_mod",
    )(*x_args, g.reshape(1, D_MODEL), mod)


def _resid_normmod_kernel(*refs, n_x, gate_idx, shift_idx, scale_idx):
    f_ref, gmod_ref, g_ref, nmod_ref, xo_ref, h_ref = refs[n_x:]
    x = _load_stream(refs[:n_x]) + gmod_ref[0, gate_idx:gate_idx + 1, :] * f_ref[...]
    xo_ref[...] = x
    h_ref[...] = _modulated_norm(x, g_ref, nmod_ref, shift_idx, scale_idx)


def _resid_normmod(x, f, gate_mod, gate_idx, g, norm_mod, shift_idx, scale_idx):
    x_specs, x_args = _stream_specs(x)
    row = pl.BlockSpec((ROW_TILE, D_MODEL), lambda i: (i, 0))
    mod = pl.BlockSpec((1, 6, D_MODEL), lambda i: (_mod_row(i, ROW_TILE), 0, 0))
    return pl.pallas_call(
        functools.partial(_resid_normmod_kernel, n_x=len(x_args), gate_idx=gate_idx,
                          shift_idx=shift_idx, scale_idx=scale_idx),
        out_shape=(jax.ShapeDtypeStruct((N_TOK, D_MODEL), F32), jax.ShapeDtypeStruct((N_TOK, D_MODEL), BF16)),
        grid=(N_TOK // ROW_TILE,),
        in_specs=x_specs + [row, mod, pl.BlockSpec((1, D_MODEL), lambda i: (0, 0)), mod],
        out_specs=(row, row),
        compiler_params=_cparams(("arbitrary",), 12 * ROW_TILE * D_MODEL * 4),
        name="resid_norm_mod",
    )(*x_args, f, gate_mod, g.reshape(1, D_MODEL), norm_mod)


def _resid_split_kernel(x_ref, f_ref, gmod_ref, op_ref, os_ref, *, gate_idx, prompt_blocks):
    x = x_ref[...] + gmod_ref[0, gate_idx:gate_idx + 1, :] * f_ref[...]
    i = pl.program_id(0)

    @pl.when(i < prompt_blocks)
    def _():
        op_ref[...] = x

    @pl.when(i >= prompt_blocks)
    def _():
        os_ref[...] = x


def _resid_split(x, f, gate_mod, gate_idx):
    tm = ROW_TILE
    pb = N_PROMPT // tm
    row = pl.BlockSpec((tm, D_MODEL), lambda i: (i, 0))
    return pl.pallas_call(
        functools.partial(_resid_split_kernel, gate_idx=gate_idx, prompt_blocks=pb),
        out_shape=(jax.ShapeDtypeStruct((N_PROMPT, D_MODEL), F32), jax.ShapeDtypeStruct((N_SAMPLE, D_MODEL), F32)),
        grid=(N_TOK // tm,),
        in_specs=[row, row, pl.BlockSpec((1, 6, D_MODEL), lambda i: (_mod_row(i, tm), 0, 0))],
        out_specs=(pl.BlockSpec((tm, D_MODEL), lambda i: (jnp.minimum(i, pb - 1), 0)),
                   pl.BlockSpec((tm, D_MODEL), lambda i: (jnp.maximum(i - pb, 0), 0))),
        compiler_params=_cparams(("arbitrary",), 10 * tm * D_MODEL * 4),
        name="resid_out",
    )(x, f, gate_mod)


def _mm_kernel(a_ref, b_ref, o_ref, *scratch, nk, relu2):
    acc_ref = scratch[0] if scratch else o_ref
    k = pl.program_id(2)

    def prod():
        return jnp.dot(a_ref[...], b_ref[...].astype(BF16), preferred_element_type=F32)

    def finish(r):
        if relu2:
            r = jnp.square(jnp.maximum(r, 0.0))
        o_ref[...] = r.astype(o_ref.dtype)

    if nk == 1:
        finish(prod())
        return

    @pl.when(k == 0)
    def _():
        acc_ref[...] = prod()

    @pl.when(jnp.logical_and(k > 0, k < nk - 1))
    def _():
        acc_ref[...] += prod()

    @pl.when(k == nk - 1)
    def _():
        finish(acc_ref[...] + prod())


def _mm_ws_kernel(a_ref, b_ref, o_ref, *scratch, relu2):
    if scratch:
        bw_ref = scratch[0]

        @pl.when(pl.program_id(1) == 0)
        def _():
            bw_ref[...] = b_ref[...].astype(BF16)

        b = bw_ref[...]
    else:
        b = b_ref[...]
    r = jnp.dot(a_ref[...], b, preferred_element_type=F32)
    if relu2:
        r = jnp.square(jnp.maximum(r, 0.0))
    o_ref[...] = r.astype(o_ref.dtype)


def _matmul_ws(a, b, layer, *, tm, tn, out_dtype, relu2=False, name):
    m, kdim = a.shape
    n = b.shape[-1]
    if layer is None:
        b_spec = pl.BlockSpec((kdim, tn), lambda j, i: (0, j))
    else:
        b_spec = pl.BlockSpec((None, kdim, tn), lambda j, i: (layer, 0, j))
    cast = b.dtype != jnp.dtype(BF16)
    o_bytes = jnp.dtype(out_dtype).itemsize
    vmem = (2 * (tm * kdim * 2 + kdim * tn * b.dtype.itemsize + tm * tn * o_bytes)
            + cast * kdim * tn * 2 + 2 * tm * tn * 4)
    return pl.pallas_call(
        functools.partial(_mm_ws_kernel, relu2=relu2),
        out_shape=jax.ShapeDtypeStruct((m, n), out_dtype),
        grid=(n // tn, m // tm),
        in_specs=[pl.BlockSpec((tm, kdim), lambda j, i: (i, 0)), b_spec],
        out_specs=pl.BlockSpec((tm, tn), lambda j, i: (i, j)),
        scratch_shapes=[pltpu.VMEM((kdim, tn), BF16)] if cast else [],
        compiler_params=_cparams(("parallel", "arbitrary"), vmem),
        name=name,
    )(a, b)


def _matmul(a, b, layer, *, tm, tn, tk, out_dtype, relu2=False, name):
    m, kdim = a.shape
    n = b.shape[-1]
    nk = kdim // tk
    if layer is None:
        b_spec = pl.BlockSpec((tk, tn), lambda i, j, k: (k, j))
    else:
        b_spec = pl.BlockSpec((None, tk, tn), lambda i, j, k: (layer, k, j))
    o_bytes = jnp.dtype(out_dtype).itemsize
    needs_acc = jnp.dtype(out_dtype) != jnp.dtype(F32)
    vmem = (2 * (tm * tk * 2 + tk * tn * b.dtype.itemsize + tm * tn * o_bytes)
            + needs_acc * tm * tn * 4 + 2 * tk * tn * 2 + 2 * tm * tk * 2)
    return pl.pallas_call(
        functools.partial(_mm_kernel, nk=nk, relu2=relu2),
        out_shape=jax.ShapeDtypeStruct((m, n), out_dtype),
        grid=(m // tm, n // tn, nk),
        in_specs=[pl.BlockSpec((tm, tk), lambda i, j, k: (i, k)), b_spec],
        out_specs=pl.BlockSpec((tm, tn), lambda i, j, k: (i, j)),
        scratch_shapes=[pltpu.VMEM((tm, tn), F32)] if needs_acc else [],
        compiler_params=_cparams(("parallel", "parallel", "arbitrary"), vmem),
        name=name,
    )(a, b)


def _rope128(r, cos, sa, sb):
    return r * cos + pltpu.roll(r, 96, 1) * sa + pltpu.roll(r, 32, 1) * sb


def _q_kernel(ql_ref, ga_ref, w_ref, gn_ref, gr_ref, cos_ref, sa_ref, sb_ref, o_ref, qn_ref, *, heads_per_tile):
    @pl.when(pl.program_id(1) == 0)
    def _():
        qn_ref[...] = _rms(ql_ref[...], ga_ref[...], Q_LORA).astype(BF16)

    cos, sa, sb = cos_ref[...], sa_ref[...], sb_ref[...]
    qn = qn_ref[...]
    for hh in range(heads_per_tile):
        c0 = hh * QK_PAD
        acc = jnp.dot(qn, w_ref[:, c0:c0 + QK_PAD], preferred_element_type=F32)
        nope = acc[:, :QK_NOPE]
        rope = acc[:, QK_NOPE:]
        nope = _rms(nope, gn_ref[...], QK_NOPE) * ATTN_SCALE
        rope = _rope128(_rms(rope, gr_ref[...], QK_ROPE), cos, sa, sb) * ATTN_SCALE
        o_ref[:, c0:c0 + QK_NOPE] = nope.astype(BF16)
        o_ref[:, c0 + QK_NOPE:c0 + QK_PAD] = rope.astype(BF16)


def _rope_specs(tm, nargs):
    blocks_per_seq = DEC_SEQ // tm

    def idx(i, *_):
        return (jnp.minimum(i * tm // N_PROMPT, 1), i % blocks_per_seq, 0)

    return [pl.BlockSpec((None, tm, V7X_LANES), idx) for _ in range(nargs)]


def _q_proj(proj, ga, w_qb_p, gn, gr128, rope_tabs):
    tm, tn = 512, 1024
    n = MLA_HEADS * QK_PAD
    return pl.pallas_call(
        functools.partial(_q_kernel, heads_per_tile=tn // QK_PAD),
        out_shape=jax.ShapeDtypeStruct((N_TOK, n), BF16),
        grid=(N_TOK // tm, n // tn),
        in_specs=[
            pl.BlockSpec((tm, Q_LORA), lambda i, j: (i, 0)),
            pl.BlockSpec((1, Q_LORA), lambda i, j: (0, 0)),
            pl.BlockSpec((Q_LORA, tn), lambda i, j: (0, j)),
            pl.BlockSpec((1, QK_NOPE), lambda i, j: (0, 0)),
            pl.BlockSpec((1, V7X_LANES), lambda i, j: (0, 0)),
        ] + _rope_specs(tm, 3),
        out_specs=pl.BlockSpec((tm, tn), lambda i, j: (i, j)),
        scratch_shapes=[pltpu.VMEM((tm, Q_LORA), BF16)],
        compiler_params=_cparams(
            ("parallel", "arbitrary"),
            2 * (tm * Q_LORA * 4 + Q_LORA * tn * 2 + tm * tn * 2) + tm * Q_LORA * 2 + 3 * tm * tn * 4),
        name="mla_q_proj",
    )(proj, ga, w_qb_p, gn, gr128, *rope_tabs)


def _kvnorm_kernel(c_ref, r_ref, gc_ref, gk_ref, cos_ref, sa_ref, sb_ref, ckv_ref, kr_ref, krr_ref):
    ckv_ref[...] = _rms(c_ref[...], gc_ref[...], KV_LORA)
    rn = _rms(r_ref[...], gk_ref[...], QK_ROPE)
    kr_ref[...] = rn
    krr_ref[...] = _rope128(rn, cos_ref[...], sa_ref[...], sb_ref[...]).astype(BF16)


def _kv_norm(proj, gc, gk128, rope_tabs):
    tm = 512
    return pl.pallas_call(
        _kvnorm_kernel,
        out_shape=(jax.ShapeDtypeStruct((N_TOK, KV_LORA), F32),
                   jax.ShapeDtypeStruct((N_TOK, V7X_LANES), F32),
                   jax.ShapeDtypeStruct((N_TOK, V7X_LANES), BF16)),
        grid=(N_TOK // tm,),
        in_specs=[
            pl.BlockSpec((tm, KV_LORA), lambda i: (i, P_KV // KV_LORA)),
            pl.BlockSpec((tm, V7X_LANES), lambda i: (i, P_KROPE // V7X_LANES)),
            pl.BlockSpec((1, KV_LORA), lambda i: (0, 0)),
            pl.BlockSpec((1, V7X_LANES), lambda i: (0, 0)),
        ] + _rope_specs(tm, 3),
        out_specs=(pl.BlockSpec((tm, KV_LORA), lambda i: (i, 0)),
                   pl.BlockSpec((tm, V7X_LANES), lambda i: (i, 0)),
                   pl.BlockSpec((tm, V7X_LANES), lambda i: (i, 0))),
        compiler_params=_cparams(("parallel",), 8 * tm * (KV_LORA + 4 * V7X_LANES) * 4),
        name="mla_kv_norm",
    )(proj, proj, gc, gk128, *rope_tabs)


def _kvup_kernel(c_ref, w_ref, gn_ref, o_ref, *, heads_per_tile, k_tiles):
    c = c_ref[...].astype(BF16)
    pair = 2 * QK_NOPE

    @pl.when(pl.program_id(1) < k_tiles)
    def _():
        for pp in range(heads_per_tile // 2):
            acc = jnp.dot(c, w_ref[:, pp * pair:(pp + 1) * pair], preferred_element_type=F32)
            for hh in range(2):
                c0 = pp * pair + hh * QK_NOPE
                o_ref[:, c0:c0 + QK_NOPE] = _rms(
                    acc[:, hh * QK_NOPE:(hh + 1) * QK_NOPE], gn_ref[...], QK_NOPE).astype(BF16)

    @pl.when(pl.program_id(1) >= k_tiles)
    def _():
        o_ref[...] = jnp.dot(c, w_ref[...], preferred_element_type=F32).astype(BF16)


def _kv_up(ckv_all, w_kvb_p, gn):
    tm, tn = 512, 1024
    m = ckv_all.shape[0]
    n = 2 * MLA_OUT
    return pl.pallas_call(
        functools.partial(_kvup_kernel, heads_per_tile=tn // QK_NOPE, k_tiles=MLA_OUT // tn),
        out_shape=jax.ShapeDtypeStruct((m, n), BF16),
        grid=(m // tm, n // tn),
        in_specs=[
            pl.BlockSpec((tm, KV_LORA), lambda i, j: (i, 0)),
            pl.BlockSpec((KV_LORA, tn), lambda i, j: (0, j)),
            pl.BlockSpec((1, QK_NOPE), lambda i, j: (0, 0)),
        ],
        out_specs=pl.BlockSpec((tm, tn), lambda i, j: (i, j)),
        compiler_params=_cparams(
            ("parallel", "parallel"),
            2 * (tm * KV_LORA * 4 + KV_LORA * tn * 2 + tm * tn * 2) + 3 * tm * tn * 4),
        name="mla_kv_up",
    )(ckv_all, w_kvb_p, gn)


def _attn_kernel(*refs, n_seg):
    q_ref = refs[0]
    seg_refs = [refs[1 + 3 * s:4 + 3 * s] for s in range(n_seg)]
    g_ref = refs[1 + 3 * n_seg]
    o_ref, oacc_ref = refs[-2:]
    for h in range(MLA_HEADS):
        qh = q_ref[:, h * QK_PAD:(h + 1) * QK_PAD]
        scores = []
        for kn_ref, kr_ref, _ in seg_refs:
            kh = jnp.concatenate([kn_ref[:, h * QK_NOPE:(h + 1) * QK_NOPE], kr_ref[...]], axis=1)
            scores.append(lax.dot_general(qh, kh, (((1,), (1,)), ((), ())), preferred_element_type=F32))
        m = scores[0].max(axis=-1, keepdims=True)
        for s in scores[1:]:
            m = jnp.maximum(m, s.max(axis=-1, keepdims=True))
        denom = None
        out = None
        for s, (_, _, v_ref) in zip(scores, seg_refs):
            p = jnp.exp(s - m)
            ps = p.sum(axis=-1, keepdims=True)
            pv = jnp.dot(p.astype(BF16), v_ref[:, h * V_HEAD:(h + 1) * V_HEAD], preferred_element_type=F32)
            denom = ps if denom is None else denom + ps
            out = pv if out is None else out + pv
        oacc_ref[:, h * V_HEAD:(h + 1) * V_HEAD] = out / denom
    o_ref[...] = _rms(oacc_ref[...], g_ref[...], MLA_OUT).astype(BF16)


def _mix_alias(mix, args, in_specs):
    if mix is None:
        return {}
    args.append(mix)
    in_specs.append(pl.BlockSpec(memory_space=pl.ANY))
    return {len(args) - 1: 0}


def _attention(q, kv, kr, g_out, mix, *, n_batch, t_q, segs, q_row0):
    tq = 256
    qb = t_q // tq
    in_specs = [pl.BlockSpec((tq, MLA_HEADS * QK_PAD), lambda b, i: (q_row0 // tq + b * qb + i, 0))]
    args = [q]
    vmem = 2 * tq * MLA_HEADS * QK_PAD * 2
    for rows, row0 in segs:
        blk0 = row0 // rows
        in_specs += [
            pl.BlockSpec((rows, MLA_OUT), lambda b, i, blk0=blk0: (blk0 + b, 0)),
            pl.BlockSpec((rows, V7X_LANES), lambda b, i, blk0=blk0: (blk0 + b, 0)),
            pl.BlockSpec((rows, MLA_OUT), lambda b, i, blk0=blk0: (blk0 + b, 1)),
        ]
        args += [kv, kr, kv]
        vmem += 2 * rows * (2 * MLA_OUT + V7X_LANES) * 2 + 12 * tq * rows * 4
    in_specs.append(pl.BlockSpec((1, MLA_OUT), lambda b, i: (0, 0)))
    args.append(g_out)
    vmem += 4 * tq * MLA_OUT * 4
    aliases = _mix_alias(mix, args, in_specs)
    return pl.pallas_call(
        functools.partial(_attn_kernel, n_seg=len(segs)),
        out_shape=jax.ShapeDtypeStruct((N_TOK, MLA_OUT + GLA_OUT), BF16),
        grid=(n_batch, qb),
        in_specs=in_specs,
        out_specs=pl.BlockSpec((tq, MLA_OUT), lambda b, i: (q_row0 // tq + b * qb + i, 0)),
        scratch_shapes=[pltpu.VMEM((tq, MLA_OUT), F32)],
        input_output_aliases=aliases,
        compiler_params=_cparams(("parallel", "arbitrary"), vmem),
        name="mla_attention_%d" % t_q,
    )(*args)


def _log_sigmoid(x):
    return -(jnp.maximum(-x, 0.0) + jnp.log(1.0 + jnp.exp(-jnp.abs(x))))


def _exact_tri_dot(keep, g):
    tri = jnp.where(keep, 1.0, 0.0).astype(BF16)
    g_hi = g.astype(BF16)
    rem = g - g_hi.astype(F32)
    g_mid = rem.astype(BF16)
    g_lo = (rem - g_mid.astype(F32)).astype(BF16)
    return (jnp.dot(tri, g_hi, preferred_element_type=F32)
            + jnp.dot(tri, g_mid, preferred_element_type=F32)
            + jnp.dot(tri, g_lo, preferred_element_type=F32))


GLA_HEAD_GROUP = 4


def _gla_kernel(q_ref, k_ref, v_ref, glr_ref, wgf_ref, wgb_ref, bgf_ref, bgb_ref, s0f_ref, s0b_ref,
                og_ref, gn_ref, *rest, n_chunks):
    o_ref, sf_ref, sb_ref, gf_s, gb_s, of_s, ob_s, st_s = rest[-8:]
    hg = GLA_HEAD_GROUP
    glr = glr_ref[...].astype(BF16)

    def gate(w_ref, b_ref):
        x = jnp.dot(glr, w_ref[...].astype(BF16), preferred_element_type=F32) + b_ref[...]
        return _log_sigmoid(x) * (1.0 / GATE_TAU)

    gf_s[...] = gate(wgf_ref, bgf_ref)
    gb_s[...] = gate(wgb_ref, bgb_ref)
    for j in range(hg):
        st_s[j] = s0f_ref[j].T
        st_s[hg + j] = s0b_ref[j].T

    row = lax.broadcasted_iota(jnp.int32, (CHUNK, CHUNK), 0)
    col = lax.broadcasted_iota(jnp.int32, (CHUNK, CHUNK), 1)
    nt = (((1,), (1,)), ((), ()))
    tn = (((0,), (0,)), ((), ()))

    keep_f = col <= row
    keep_b = col >= row

    def cumulate(n, carry):
        sl = pl.ds(pl.multiple_of(n * CHUNK, CHUNK), CHUNK)
        gf_s[sl, :] = _exact_tri_dot(keep_f, gf_s[sl, :])
        gb_s[sl, :] = _exact_tri_dot(keep_b, gb_s[sl, :])
        return carry

    lax.fori_loop(0, n_chunks, cumulate, 0)

    def step(n, carry):
        units = []
        for j in range(hg):
            units.append((n, j, j, gf_s, of_s, keep_f, CHUNK - 1, CHUNK // 2 - 1))
            units.append((n_chunks - 1 - n, j, hg + j, gb_s, ob_s, keep_b, 0, CHUNK // 2))
        staged = []
        for cn, j, slot, b_s, o_s, keep, end_idx, mid_idx in units:
            sl = pl.ds(pl.multiple_of(cn * CHUNK, CHUNK), CHUNK)
            dk = slice(j * GLA_DK, (j + 1) * GLA_DK)
            dv = slice(j * GLA_DV, (j + 1) * GLA_DV)
            b = b_s[sl, dk]
            q = q_ref[sl, dk] * GLA_SCALE
            k = k_ref[sl, dk]
            v = v_ref[sl, dv].astype(BF16)
            b_end = b[end_idx:end_idx + 1, :]
            b_mid = b[mid_idx:mid_idx + 1, :]
            qe = (q * jnp.exp(b)).astype(BF16)
            qa = (q * jnp.exp(b - b_mid)).astype(BF16)
            ka = (k * jnp.exp(b_mid - b)).astype(BF16)
            kd = (k * jnp.exp(b_end - b)).astype(BF16)
            staged.append((sl, dv, slot, o_s, keep, v, qe, qa, ka, kd, jnp.exp(b_end)))
        scores = [lax.dot_general(qa, ka, nt, preferred_element_type=F32)
                  for _, _, _, _, _, _, _, qa, ka, _, _ in staged]
        inter = [lax.dot_general(qe, st_s[slot].astype(BF16), nt, preferred_element_type=F32)
                 for _, _, slot, _, _, _, qe, _, _, _, _ in staged]
        update = [lax.dot_general(v, kd, tn, preferred_element_type=F32)
                  for _, _, _, _, _, v, _, _, _, kd, _ in staged]
        for (sl, dv, slot, o_s, keep, v, _, _, _, _, decay), a, o_inter, upd in zip(staged, scores, inter, update):
            a = jnp.where(keep, a, 0.0).astype(BF16)
            o_s[sl, dv] = jnp.dot(a, v, preferred_element_type=F32) + o_inter
            st_s[slot] = st_s[slot] * decay + upd
        return carry

    lax.fori_loop(0, n_chunks, step, 0)
    for j in range(hg):
        sf_ref[j] = st_s[j].T
        sb_ref[j] = st_s[hg + j].T
        dv = slice(j * GLA_DV, (j + 1) * GLA_DV)
        og = og_ref[:, dv]
        o = _rms(of_s[:, dv] + ob_s[:, dv], gn_ref[...], GLA_DV)
        o_ref[:, dv] = (o * (og / (1.0 + jnp.exp(-og)))).astype(BF16)


def _gla(proj, wg, bg, s0f, s0b, gn, mix, *, n_batch, t, row0):
    hg = GLA_HEAD_GROUP
    rb0 = row0 // t
    n_groups = GLA_HEADS // hg
    tok = lambda width, col0: pl.BlockSpec(
        (t, hg * width), lambda b, h, cb=col0 // (hg * width): (rb0 + b, cb + h))
    state = pl.BlockSpec((None, hg, GLA_DK, GLA_DV), lambda b, h: (b, h, 0, 0))
    in_specs = [
        tok(GLA_DK, P_GQ),
        tok(GLA_DK, P_GK),
        tok(GLA_DV, P_GV),
        pl.BlockSpec((t, V7X_LANES), lambda b, h: (rb0 + b, P_GATE // V7X_LANES)),
        pl.BlockSpec((V7X_LANES, hg * GLA_DK), lambda b, h: (0, h)),
        pl.BlockSpec((V7X_LANES, hg * GLA_DK), lambda b, h: (0, n_groups + h)),
        pl.BlockSpec((1, hg * GLA_DK), lambda b, h: (0, h)),
        pl.BlockSpec((1, hg * GLA_DK), lambda b, h: (0, n_groups + h)),
        state, state,
        tok(GLA_DV, P_OG),
        pl.BlockSpec((1, GLA_DV), lambda b, h: (0, 0)),
    ]
    args = [proj, proj, proj, proj, wg, wg, bg, bg, s0f, s0b, proj, gn]
    aliases = _mix_alias(mix, args, in_specs)
    return pl.pallas_call(
        functools.partial(_gla_kernel, n_chunks=t // CHUNK),
        out_shape=(jax.ShapeDtypeStruct((N_TOK, MLA_OUT + GLA_OUT), BF16),
                   jax.ShapeDtypeStruct((n_batch, GLA_HEADS, GLA_DK, GLA_DV), F32),
                   jax.ShapeDtypeStruct((n_batch, GLA_HEADS, GLA_DK, GLA_DV), F32)),
        grid=(n_batch, n_groups),
        in_specs=in_specs,
        input_output_aliases=aliases,
        out_specs=(pl.BlockSpec((t, hg * GLA_DV),
                                lambda b, h: (rb0 + b, MLA_OUT // (hg * GLA_DV) + h)), state, state),
        scratch_shapes=[pltpu.VMEM((t, hg * GLA_DK), F32), pltpu.VMEM((t, hg * GLA_DK), F32),
                        pltpu.VMEM((t, hg * GLA_DV), F32), pltpu.VMEM((t, hg * GLA_DV), F32),
                        pltpu.VMEM((2 * hg, GLA_DV, GLA_DK), F32)],
        compiler_params=_cparams(
            ("parallel", "parallel"),
            2 * (t * hg * (2 * GLA_DK + 2 * GLA_DV) * 4 + t * V7X_LANES * 4 + t * hg * GLA_DV * 2)
            + t * hg * (2 * GLA_DK + 2 * GLA_DV) * 4 + 6 * t * hg * GLA_DK * 4
            + 10 * hg * GLA_DK * GLA_DV * 4 + 4 * V7X_LANES * hg * GLA_DK * 2),
        name="gla_%d" % t,
    )(*args)


def _pack_w_in(w):
    z = lambda n: jnp.zeros((D_MODEL, n), w.dtype)
    return jnp.concatenate([
        w[:, :OFF_KV],
        w[:, OFF_KV:OFF_KV + KV_LORA],
        w[:, OFF_KV + KV_LORA:OFF_GQ], z(V7X_LANES - QK_ROPE),
        w[:, OFF_GATE:OFF_OG], z(V7X_LANES - 2 * GATE_RANK),
        z(P_GQ - P_GATE - V7X_LANES),
        w[:, OFF_GQ:OFF_GK], w[:, OFF_GK:OFF_GV], w[:, OFF_GV:OFF_GATE], w[:, OFF_OG:],
    ], axis=1).astype(BF16)


def _pack_w_qb(w):
    w = w.reshape(Q_LORA, MLA_HEADS, QK_NOPE + QK_ROPE)
    w = jnp.pad(w, ((0, 0), (0, 0), (0, QK_PAD - QK_NOPE - QK_ROPE)))
    return w.reshape(Q_LORA, MLA_HEADS * QK_PAD).astype(BF16)


def _pack_w_kvb(w):
    w = w.reshape(KV_LORA, MLA_HEADS, QK_NOPE + V_HEAD)
    return jnp.concatenate([w[:, :, :QK_NOPE].reshape(KV_LORA, MLA_OUT),
                            w[:, :, QK_NOPE:].reshape(KV_LORA, MLA_OUT)], axis=1).astype(BF16)


def _pack_gate(w_gf2, w_gb2, b_gf, b_gb):
    wg = jnp.zeros((V7X_LANES, 2 * GLA_QK), F32)
    wg = wg.at[:GATE_RANK, :GLA_QK].set(w_gf2).at[GATE_RANK:2 * GATE_RANK, GLA_QK:].set(w_gb2)
    return wg.astype(BF16), jnp.concatenate([b_gf, b_gb]).reshape(1, 2 * GLA_QK)


def _pad_lanes(g):
    return jnp.pad(g, (0, V7X_LANES - g.shape[0])).reshape(1, V7X_LANES)


def _rope_tables():
    rows = DEC_SEQ // GRID_W
    row = jnp.repeat(jnp.arange(rows, dtype=F32), GRID_W)
    col = jnp.tile(jnp.arange(GRID_W, dtype=F32), rows)
    n_freq = QK_ROPE // 4
    inv = jnp.power(ROPE_BASE, -jnp.arange(n_freq, dtype=F32) / n_freq)
    ang = jnp.concatenate([row[:, None] * inv, col[:, None] * inv], axis=-1)
    cos, sin = jnp.cos(ang), jnp.sin(ang)
    half = QK_ROPE // 2
    zpad = jnp.zeros((DEC_SEQ, V7X_LANES - QK_ROPE), F32)
    zhalf = jnp.zeros((DEC_SEQ, half), F32)
    cos_t = jnp.concatenate([cos, cos, zpad], axis=1)
    sa_t = jnp.concatenate([-sin, zhalf, zpad], axis=1)
    sb_t = jnp.concatenate([zhalf, sin, zpad], axis=1)
    ident = jnp.concatenate([jnp.ones((DEC_SEQ, QK_ROPE), F32), zpad], axis=1)
    zeros = jnp.zeros((DEC_SEQ, V7X_LANES), F32)
    return (jnp.stack([ident, cos_t]), jnp.stack([zeros, sa_t]), jnp.stack([zeros, sb_t]))


def kernel(x_prompt, x_sample, c, cache_ckv, cache_krope, state_gla_fwd, state_gla_bwd, c_ctx, w_ada, b_ada, norm1, norm2, w_in, q_a_norm, w_qb, kv_a_norm, w_kvb, q_norm_nope, q_norm_rope, k_norm_nope, k_norm_rope, w_gf2, b_gf, w_gb2, b_gb, gla_norm, mla_out_norm, w_o, w_up, w_down):
    x = (x_prompt.reshape(N_PROMPT, D_MODEL), x_sample.reshape(N_SAMPLE, D_MODEL))
    cvec = jnp.concatenate([c_ctx[None, :], c, jnp.zeros((N_MOD_ROWS - 1 - DEC_BATCH, D_MODEL), F32)], axis=0)
    mods = _ada(cvec, w_ada, b_ada).reshape(DEPTH, N_MOD_ROWS, 6, D_MODEL)
    rope_tabs = _rope_tables()
    zero_state = jnp.zeros((BATCH, GLA_HEADS, GLA_DK, GLA_DV), F32)

    ckvs, kropes, sfs, sbs = [], [], [], []
    h = _normmod(x, norm1[0], mods[0], 0, 1)
    for l in range(DEPTH):
        mod = mods[l]
        proj = _matmul_ws(h, _pack_w_in(w_in[l]), None, tm=1024, tn=1024, out_dtype=F32, name="in_proj")

        q = _q_proj(proj, q_a_norm[l].reshape(1, Q_LORA), _pack_w_qb(w_qb[l]),
                    q_norm_nope[l].reshape(1, QK_NOPE), _pad_lanes(q_norm_rope[l]), rope_tabs)
        ckv, kr, krr = _kv_norm(proj, kv_a_norm[l].reshape(1, KV_LORA), _pad_lanes(k_norm_rope[l]), rope_tabs)
        ckv_all = jnp.concatenate([ckv, cache_ckv[:, l].reshape(N_CACHE, KV_LORA)], axis=0)
        kv = _kv_up(ckv_all, _pack_w_kvb(w_kvb[l]), k_norm_nope[l].reshape(1, QK_NOPE))
        kr_cache = jnp.pad(cache_krope[:, l].reshape(N_CACHE, QK_ROPE),
                           ((0, 0), (0, V7X_LANES - QK_ROPE))).astype(BF16)
        kr_all = jnp.concatenate([krr, kr_cache], axis=0)
        g_mla = mla_out_norm[l].reshape(1, MLA_OUT)
        mix = _attention(q, kv, kr_all, g_mla, None, n_batch=BATCH, t_q=SEQ, q_row0=0,
                         segs=[(SEQ, 0)])
        mix = _attention(q, kv, kr_all, g_mla, mix, n_batch=DEC_BATCH, t_q=DEC_SEQ, q_row0=N_PROMPT,
                         segs=[(DEC_SEQ, N_PROMPT), (PAST_LEN, N_TOK)])

        wg, bg = _pack_gate(w_gf2[l], w_gb2[l], b_gf[l], b_gb[l])
        gn = gla_norm[l].reshape(1, GLA_DV)
        mix, s_f, s_b = _gla(proj, wg, bg, zero_state, zero_state, gn, mix, n_batch=BATCH, t=SEQ, row0=0)
        mix, _, _ = _gla(proj, wg, bg, state_gla_fwd[:, l], state_gla_bwd[:, l], gn, mix,
                         n_batch=DEC_BATCH, t=DEC_SEQ, row0=N_PROMPT)

        f = _matmul(mix, w_o, l, tm=4096, tn=1024, tk=512, out_dtype=BF16, name="out_proj")
        x, h2 = _resid_normmod(x, f, mod, 2, norm2[l], mod, 3, 4)

        act = _matmul_ws(h2, w_up, l, tm=512, tn=1024, out_dtype=BF16, relu2=True, name="mlp_up")
        f = _matmul(act, w_down, l, tm=4096, tn=1024, tk=512, out_dtype=BF16, name="mlp_down")
        if l + 1 < DEPTH:
            x, h = _resid_normmod(x, f, mod, 5, norm1[l + 1], mods[l + 1], 0, 1)
        else:
            y_prompt, y_sample = _resid_split(x, f, mod, 5)

        ckvs.append(ckv[:N_PROMPT].reshape(BATCH, SEQ, KV_LORA))
        kropes.append(kr[:N_PROMPT, :QK_ROPE].reshape(BATCH, SEQ, QK_ROPE))
        sfs.append(s_f)
        sbs.append(s_b)

    return (y_prompt.reshape(BATCH, SEQ, D_MODEL),
            y_sample.reshape(DEC_BATCH, DEC_SEQ, D_MODEL),
            jnp.stack(ckvs, axis=1), jnp.stack(kropes, axis=1),
            jnp.stack(sfs, axis=1), jnp.stack(sbs, axis=1))
```

```python
import functools

import jax
import jax.numpy as jnp
from jax import lax
from jax.experimental import pallas as pl
from jax.experimental.pallas import tpu as pltpu

D_MODEL = 4096
BATCH = 16
SEQ = 256
DEPTH = 2
DEC_BATCH = 4
DEC_SEQ = 1024
PAST_LEN = 512
GRID_W = 64
MLA_HEADS = 16
QK_NOPE = 128
QK_ROPE = 64
V_HEAD = 128
Q_LORA = 1024
KV_LORA = 512
GLA_HEADS = 8
GLA_DK = 128
GLA_DV = 256
GATE_RANK = 16
GATE_TAU = 16.0
CHUNK = 64
D_FF = 4 * D_MODEL
ROPE_BASE = 10000.0
EPS = 1e-6

MLA_OUT = MLA_HEADS * V_HEAD
GLA_QK = GLA_HEADS * GLA_DK
GLA_OUT = GLA_HEADS * GLA_DV
OFF_KV = Q_LORA
OFF_GQ = OFF_KV + KV_LORA + QK_ROPE
OFF_GK = OFF_GQ + GLA_QK
OFF_GV = OFF_GK + GLA_QK
OFF_GATE = OFF_GV + GLA_OUT
OFF_OG = OFF_GATE + 2 * GATE_RANK
N_IN = OFF_OG + GLA_OUT

F32 = jnp.float32
BF16 = jnp.bfloat16

V7X_LANES = 128
V7X_SCOPED_VMEM_BYTES = 60000 * 1024

N_PROMPT = BATCH * SEQ
N_SAMPLE = DEC_BATCH * DEC_SEQ
N_TOK = N_PROMPT + N_SAMPLE
N_CACHE = DEC_BATCH * PAST_LEN
N_MOD_ROWS = 8

P_KV = 1024
P_KROPE = P_KV + KV_LORA
P_GATE = P_KROPE + V7X_LANES
P_GQ = 2048
P_GK = P_GQ + GLA_QK
P_GV = P_GK + GLA_QK
P_OG = P_GV + GLA_OUT
P_TOTAL = P_OG + GLA_OUT
QK_PAD = 2 * V7X_LANES

ATTN_SCALE = float((QK_NOPE + QK_ROPE) ** -0.5)
GLA_SCALE = float(GLA_DK ** -0.5)


def _cparams(sem, vmem_bytes):
    return pltpu.CompilerParams(dimension_semantics=sem,
                                vmem_limit_bytes=min(int(vmem_bytes), V7X_SCOPED_VMEM_BYTES))


def _mod_row(row_block, rows_per_block):
    return jnp.maximum(row_block * rows_per_block // DEC_SEQ - (N_PROMPT // DEC_SEQ - 1), 0)


def _rms(x, g, n):
    ms = jnp.sum(x * x, axis=-1, keepdims=True) * (1.0 / n)
    return x * lax.rsqrt(ms + EPS) * g


def _ada_kernel(c_ref, w_ref, b_ref, o_ref):
    k = pl.program_id(2)

    @pl.when(k == 0)
    def _():
        o_ref[...] = jnp.broadcast_to(b_ref[...], o_ref.shape)

    c = c_ref[...]
    s = (c / (1.0 + jnp.exp(-c))).astype(BF16)
    o_ref[...] += jnp.dot(s, w_ref[...].astype(BF16), preferred_element_type=F32)


def _ada(cvec, w_ada, b_ada):
    tk, tn = 1024, 2048
    n_out = 6 * D_MODEL
    return pl.pallas_call(
        _ada_kernel,
        out_shape=jax.ShapeDtypeStruct((DEPTH, N_MOD_ROWS, n_out), F32),
        grid=(DEPTH, n_out // tn, D_MODEL // tk),
        in_specs=[
            pl.BlockSpec((N_MOD_ROWS, tk), lambda l, j, k: (0, k)),
            pl.BlockSpec((None, tk, tn), lambda l, j, k: (l, k, j)),
            pl.BlockSpec((None, 1, tn), lambda l, j, k: (l, 0, j)),
        ],
        out_specs=pl.BlockSpec((None, N_MOD_ROWS, tn), lambda l, j, k: (l, 0, j)),
        compiler_params=_cparams(("parallel", "parallel", "arbitrary"), 3 * tk * tn * 4),
        name="ada_mod",
    )(cvec, w_ada, b_ada.reshape(DEPTH, 1, n_out))


ROW_TILE = 256


def _stream_specs(x, tm=ROW_TILE, tn=D_MODEL):
    col = (lambda idx: idx[1]) if tn != D_MODEL else (lambda idx: 0)
    if not isinstance(x, tuple):
        return [pl.BlockSpec((tm, tn), lambda *idx: (idx[0], col(idx)))], [x]
    pb = N_PROMPT // tm
    last_col = D_MODEL // tn - 1
    return ([pl.BlockSpec((tm, tn), lambda *idx: (jnp.minimum(idx[0], pb - 1),
                                                  jnp.where(idx[0] < pb, col(idx), last_col))),
             pl.BlockSpec((tm, tn), lambda *idx: (jnp.maximum(idx[0] - pb, 0),
                                                  jnp.where(idx[0] < pb, 0, col(idx))))], list(x))


def _load_stream(x_refs, tm=ROW_TILE):
    if len(x_refs) == 1:
        return x_refs[0][...]
    in_prompt = pl.program_id(0) < N_PROMPT // tm
    return jnp.where(in_prompt, x_refs[0][...], x_refs[1][...])


def _modulated_norm(x, g_ref, mod_ref, shift_idx, scale_idx):
    y = _rms(x, g_ref[...], D_MODEL)
    sc = mod_ref[0, scale_idx:scale_idx + 1, :]
    sh = mod_ref[0, shift_idx:shift_idx + 1, :]
    return (y * (1.0 + sc) + sh).astype(BF16)


def _normmod_kernel(*refs, n_x, shift_idx, scale_idx):
    g_ref, mod_ref, o_ref = refs[n_x:]
    o_ref[...] = _modulated_norm(_load_stream(refs[:n_x]), g_ref, mod_ref, shift_idx, scale_idx)


def _normmod(x, g, mod, shift_idx, scale_idx):
    x_specs, x_args = _stream_specs(x)
    row = pl.BlockSpec((ROW_TILE, D_MODEL), lambda i: (i, 0))
    return pl.pallas_call(
        functools.partial(_normmod_kernel, n_x=len(x_args), shift_idx=shift_idx, scale_idx=scale_idx),
        out_shape=jax.ShapeDtypeStruct((N_TOK, D_MODEL), BF16),
        grid=(N_TOK // ROW_TILE,),
        in_specs=x_specs + [
            pl.BlockSpec((1, D_MODEL), lambda i: (0, 0)),
            pl.BlockSpec((1, 6, D_MODEL), lambda i: (_mod_row(i, ROW_TILE), 0, 0)),
        ],
        out_specs=row,
        compiler_params=_cparams(("arbitrary",), 8 * ROW_TILE * D_MODEL * 4),
        name="norm_mod",
    )(*x_args, g.reshape(1, D_MODEL), mod)


def _mm_resid_kernel(a_ref, b_ref, *rest, nk, n_x, tm, gate_idx):
    x_refs, mod_refs, o_ref = rest[:n_x], rest[n_x:-1], rest[-1]
    k = pl.program_id(2)

    def prod():
        return jnp.dot(a_ref[...], b_ref[...].astype(BF16), preferred_element_type=F32)

    @pl.when(k == 0)
    def _():
        o_ref[...] = prod()

    @pl.when(jnp.logical_and(k > 0, k < nk - 1))
    def _():
        o_ref[...] += prod()

    @pl.when(k == nk - 1)
    def _():
        upd = o_ref[...] + prod()
        x = _load_stream(x_refs, tm)
        rows = tm // len(mod_refs)
        for part, mod_ref in enumerate(mod_refs):
            sl = slice(part * rows, (part + 1) * rows)
            o_ref[sl, :] = x[sl, :] + mod_ref[0, gate_idx:gate_idx + 1, :] * upd[sl, :]


def _matmul_resid(a, b, layer, x, mod, gate_idx, *, name):
    tm, tn, tk = (1024 if isinstance(x, tuple) else 2048), 1024, 1024
    m, kdim = a.shape
    n = b.shape[-1]
    nk = kdim // tk
    x_specs, x_args = _stream_specs(x, tm, tn)
    n_mod = tm // DEC_SEQ
    mod_specs = [pl.BlockSpec((1, 6, tn), lambda i, j, k, p=p: (_mod_row(i * n_mod + p, DEC_SEQ), 0, j))
                 for p in range(n_mod)]
    vmem = (2 * (tm * tk * 2 + tk * tn * b.dtype.itemsize + (1 + len(x_args)) * tm * tn * 4)
            + 2 * tk * tn * 2 + 2 * tm * tk * 2)
    return pl.pallas_call(
        functools.partial(_mm_resid_kernel, nk=nk, n_x=len(x_args), tm=tm, gate_idx=gate_idx),
        out_shape=jax.ShapeDtypeStruct((m, n), F32),
        grid=(m // tm, n // tn, nk),
        in_specs=[pl.BlockSpec((tm, tk), lambda i, j, k: (i, k)),
                  pl.BlockSpec((None, tk, tn), lambda i, j, k: (layer, k, j))] + x_specs + mod_specs,
        out_specs=pl.BlockSpec((tm, tn), lambda i, j, k: (i, j)),
        compiler_params=_cparams(("arbitrary", "arbitrary", "arbitrary"), vmem),
        name=name,
    )(a, b, *x_args, *([mod] * n_mod))


def _mm_ws_kernel(a_ref, b_ref, o_ref, *scratch, relu2):
    if scratch:
        bw_ref = scratch[0]

        @pl.when(pl.program_id(1) == 0)
        def _():
            bw_ref[...] = b_ref[...].astype(BF16)

        b = bw_ref[...]
    else:
        b = b_ref[...]
    r = jnp.dot(a_ref[...], b, preferred_element_type=F32)
    if relu2:
        r = jnp.square(jnp.maximum(r, 0.0))
    o_ref[...] = r.astype(o_ref.dtype)


def _matmul_ws(a, b, layer, *, tm, tn, out_dtype, relu2=False, name):
    m, kdim = a.shape
    n = b.shape[-1]
    if layer is None:
        b_spec = pl.BlockSpec((kdim, tn), lambda j, i: (0, j))
    else:
        b_spec = pl.BlockSpec((None, kdim, tn), lambda j, i: (layer, 0, j))
    cast = b.dtype != jnp.dtype(BF16)
    o_bytes = jnp.dtype(out_dtype).itemsize
    vmem = (2 * (tm * kdim * 2 + kdim * tn * b.dtype.itemsize + tm * tn * o_bytes)
            + cast * kdim * tn * 2 + 2 * tm * tn * 4)
    return pl.pallas_call(
        functools.partial(_mm_ws_kernel, relu2=relu2),
        out_shape=jax.ShapeDtypeStruct((m, n), out_dtype),
        grid=(n // tn, m // tm),
        in_specs=[pl.BlockSpec((tm, kdim), lambda j, i: (i, 0)), b_spec],
        out_specs=pl.BlockSpec((tm, tn), lambda j, i: (i, j)),
        scratch_shapes=[pltpu.VMEM((kdim, tn), BF16)] if cast else [],
        compiler_params=_cparams(("parallel", "arbitrary"), vmem),
        name=name,
    )(a, b)


def _rope128(r, cos, sa, sb):
    return r * cos + pltpu.roll(r, 96, 1) * sa + pltpu.roll(r, 32, 1) * sb


def _q_kernel(ql_ref, ga_ref, w_ref, gn_ref, gr_ref, cos_ref, sa_ref, sb_ref, o_ref, qn_ref, *, heads_per_tile):
    @pl.when(pl.program_id(1) == 0)
    def _():
        qn_ref[...] = _rms(ql_ref[...], ga_ref[...], Q_LORA).astype(BF16)

    cos, sa, sb = cos_ref[...], sa_ref[...], sb_ref[...]
    qn = qn_ref[...]
    for hh in range(heads_per_tile):
        c0 = hh * QK_PAD
        acc = jnp.dot(qn, w_ref[:, c0:c0 + QK_PAD], preferred_element_type=F32)
        nope = acc[:, :QK_NOPE]
        rope = acc[:, QK_NOPE:]
        nope = _rms(nope, gn_ref[...], QK_NOPE) * ATTN_SCALE
        rope = _rope128(_rms(rope, gr_ref[...], QK_ROPE), cos, sa, sb) * ATTN_SCALE
        o_ref[:, c0:c0 + QK_NOPE] = nope.astype(BF16)
        o_ref[:, c0 + QK_NOPE:c0 + QK_PAD] = rope.astype(BF16)


def _rope_specs(tm, nargs):
    blocks_per_seq = DEC_SEQ // tm

    def idx(i, *_):
        return (jnp.minimum(i * tm // N_PROMPT, 1), i % blocks_per_seq, 0)

    return [pl.BlockSpec((None, tm, V7X_LANES), idx) for _ in range(nargs)]


def _q_proj(proj, ga, w_qb_p, gn, gr128, rope_tabs):
    tm, tn = 512, 1024
    n = MLA_HEADS * QK_PAD
    return pl.pallas_call(
        functools.partial(_q_kernel, heads_per_tile=tn // QK_PAD),
        out_shape=jax.ShapeDtypeStruct((N_TOK, n), BF16),
        grid=(N_TOK // tm, n // tn),
        in_specs=[
            pl.BlockSpec((tm, Q_LORA), lambda i, j: (i, 0)),
            pl.BlockSpec((1, Q_LORA), lambda i, j: (0, 0)),
            pl.BlockSpec((Q_LORA, tn), lambda i, j: (0, j)),
            pl.BlockSpec((1, QK_NOPE), lambda i, j: (0, 0)),
            pl.BlockSpec((1, V7X_LANES), lambda i, j: (0, 0)),
        ] + _rope_specs(tm, 3),
        out_specs=pl.BlockSpec((tm, tn), lambda i, j: (i, j)),
        scratch_shapes=[pltpu.VMEM((tm, Q_LORA), BF16)],
        compiler_params=_cparams(
            ("parallel", "arbitrary"),
            2 * (tm * Q_LORA * 4 + Q_LORA * tn * 2 + tm * tn * 2) + tm * Q_LORA * 2 + 3 * tm * tn * 4),
        name="mla_q_proj",
    )(proj, ga, w_qb_p, gn, gr128, *rope_tabs)


def _kvnorm_kernel(c_ref, r_ref, gc_ref, gk_ref, cos_ref, sa_ref, sb_ref, ckv_ref, kr_ref, krr_ref):
    ckv_ref[...] = _rms(c_ref[...], gc_ref[...], KV_LORA)
    rn = _rms(r_ref[...], gk_ref[...], QK_ROPE)
    kr_ref[...] = rn
    krr_ref[...] = _rope128(rn, cos_ref[...], sa_ref[...], sb_ref[...]).astype(BF16)


def _kv_norm(proj, gc, gk128, rope_tabs):
    tm = 512
    return pl.pallas_call(
        _kvnorm_kernel,
        out_shape=(jax.ShapeDtypeStruct((N_TOK, KV_LORA), F32),
                   jax.ShapeDtypeStruct((N_TOK, V7X_LANES), F32),
                   jax.ShapeDtypeStruct((N_TOK, V7X_LANES), BF16)),
        grid=(N_TOK // tm,),
        in_specs=[
            pl.BlockSpec((tm, KV_LORA), lambda i: (i, P_KV // KV_LORA)),
            pl.BlockSpec((tm, V7X_LANES), lambda i: (i, P_KROPE // V7X_LANES)),
            pl.BlockSpec((1, KV_LORA), lambda i: (0, 0)),
            pl.BlockSpec((1, V7X_LANES), lambda i: (0, 0)),
        ] + _rope_specs(tm, 3),
        out_specs=(pl.BlockSpec((tm, KV_LORA), lambda i: (i, 0)),
                   pl.BlockSpec((tm, V7X_LANES), lambda i: (i, 0)),
                   pl.BlockSpec((tm, V7X_LANES), lambda i: (i, 0))),
        compiler_params=_cparams(("parallel",), 8 * tm * (KV_LORA + 4 * V7X_LANES) * 4),
        name="mla_kv_norm",
    )(proj, proj, gc, gk128, *rope_tabs)


def _kvup_kernel(c_ref, w_ref, gn_ref, o_ref, *, heads_per_tile, k_tiles):
    c = c_ref[...].astype(BF16)
    pair = 2 * QK_NOPE

    @pl.when(pl.program_id(1) < k_tiles)
    def _():
        for pp in range(heads_per_tile // 2):
            acc = jnp.dot(c, w_ref[:, pp * pair:(pp + 1) * pair], preferred_element_type=F32)
            for hh in range(2):
                c0 = pp * pair + hh * QK_NOPE
                o_ref[:, c0:c0 + QK_NOPE] = _rms(
                    acc[:, hh * QK_NOPE:(hh + 1) * QK_NOPE], gn_ref[...], QK_NOPE).astype(BF16)

    @pl.when(pl.program_id(1) >= k_tiles)
    def _():
        o_ref[...] = jnp.dot(c, w_ref[...], preferred_element_type=F32).astype(BF16)


def _kv_up(ckv_all, w_kvb_p, gn):
    tm, tn = 512, 1024
    m = ckv_all.shape[0]
    n = 2 * MLA_OUT
    return pl.pallas_call(
        functools.partial(_kvup_kernel, heads_per_tile=tn // QK_NOPE, k_tiles=MLA_OUT // tn),
        out_shape=jax.ShapeDtypeStruct((m, n), BF16),
        grid=(m // tm, n // tn),
        in_specs=[
            pl.BlockSpec((tm, KV_LORA), lambda i, j: (i, 0)),
            pl.BlockSpec((KV_LORA, tn), lambda i, j: (0, j)),
            pl.BlockSpec((1, QK_NOPE), lambda i, j: (0, 0)),
        ],
        out_specs=pl.BlockSpec((tm, tn), lambda i, j: (i, j)),
        compiler_params=_cparams(
            ("parallel", "parallel"),
            2 * (tm * KV_LORA * 4 + KV_LORA * tn * 2 + tm * tn * 2) + 3 * tm * tn * 4),
        name="mla_kv_up",
    )(ckv_all, w_kvb_p, gn)


def _attn_kernel(*refs, n_seg):
    q_ref = refs[0]
    seg_refs = [refs[1 + 3 * s:4 + 3 * s] for s in range(n_seg)]
    g_ref = refs[1 + 3 * n_seg]
    o_ref, oacc_ref = refs[-2:]
    for h in range(MLA_HEADS):
        qh = q_ref[:, h * QK_PAD:(h + 1) * QK_PAD]
        scores = []
        for kn_ref, kr_ref, _ in seg_refs:
            kh = jnp.concatenate([kn_ref[:, h * QK_NOPE:(h + 1) * QK_NOPE], kr_ref[...]], axis=1)
            scores.append(lax.dot_general(qh, kh, (((1,), (1,)), ((), ())), preferred_element_type=F32))
        m = scores[0].max(axis=-1, keepdims=True)
        for s in scores[1:]:
            m = jnp.maximum(m, s.max(axis=-1, keepdims=True))
        denom = None
        out = None
        for s, (_, _, v_ref) in zip(scores, seg_refs):
            p = jnp.exp(s - m)
            ps = p.sum(axis=-1, keepdims=True)
            pv = jnp.dot(p.astype(BF16), v_ref[:, h * V_HEAD:(h + 1) * V_HEAD], preferred_element_type=F32)
            denom = ps if denom is None else denom + ps
            out = pv if out is None else out + pv
        oacc_ref[:, h * V_HEAD:(h + 1) * V_HEAD] = out / denom
    o_ref[...] = _rms(oacc_ref[...], g_ref[...], MLA_OUT).astype(BF16)


def _mix_alias(mix, args, in_specs):
    if mix is None:
        return {}
    args.append(mix)
    in_specs.append(pl.BlockSpec(memory_space=pl.ANY))
    return {len(args) - 1: 0}


def _attention(q, kv, kr, g_out, mix, *, n_batch, t_q, segs, q_row0):
    tq = 256
    qb = t_q // tq
    in_specs = [pl.BlockSpec((tq, MLA_HEADS * QK_PAD), lambda b, i: (q_row0 // tq + b * qb + i, 0))]
    args = [q]
    vmem = 2 * tq * MLA_HEADS * QK_PAD * 2
    for rows, row0 in segs:
        blk0 = row0 // rows
        in_specs += [
            pl.BlockSpec((rows, MLA_OUT), lambda b, i, blk0=blk0: (blk0 + b, 0)),
            pl.BlockSpec((rows, V7X_LANES), lambda b, i, blk0=blk0: (blk0 + b, 0)),
            pl.BlockSpec((rows, MLA_OUT), lambda b, i, blk0=blk0: (blk0 + b, 1)),
        ]
        args += [kv, kr, kv]
        vmem += 2 * rows * (2 * MLA_OUT + V7X_LANES) * 2 + 12 * tq * rows * 4
    in_specs.append(pl.BlockSpec((1, MLA_OUT), lambda b, i: (0, 0)))
    args.append(g_out)
    vmem += 4 * tq * MLA_OUT * 4
    aliases = _mix_alias(mix, args, in_specs)
    return pl.pallas_call(
        functools.partial(_attn_kernel, n_seg=len(segs)),
        out_shape=jax.ShapeDtypeStruct((N_TOK, MLA_OUT + GLA_OUT), BF16),
        grid=(n_batch, qb),
        in_specs=in_specs,
        out_specs=pl.BlockSpec((tq, MLA_OUT), lambda b, i: (q_row0 // tq + b * qb + i, 0)),
        scratch_shapes=[pltpu.VMEM((tq, MLA_OUT), F32)],
        input_output_aliases=aliases,
        compiler_params=_cparams(("parallel", "arbitrary"), vmem),
        name="mla_attention_%d" % t_q,
    )(*args)


def _log_sigmoid(x):
    return -(jnp.maximum(-x, 0.0) + jnp.log(1.0 + jnp.exp(-jnp.abs(x))))


def _exact_tri_dot(keep, g):
    tri = jnp.where(keep, 1.0, 0.0).astype(BF16)
    g_hi = g.astype(BF16)
    rem = g - g_hi.astype(F32)
    g_mid = rem.astype(BF16)
    g_lo = (rem - g_mid.astype(F32)).astype(BF16)
    return (jnp.dot(tri, g_hi, preferred_element_type=F32)
            + jnp.dot(tri, g_mid, preferred_element_type=F32)
            + jnp.dot(tri, g_lo, preferred_element_type=F32))


GLA_HEAD_GROUP = 4


def _gla_kernel(q_ref, k_ref, v_ref, glr_ref, wgf_ref, wgb_ref, bgf_ref, bgb_ref, s0f_ref, s0b_ref,
                og_ref, gn_ref, *rest, n_chunks):
    o_ref, sf_ref, sb_ref, gf_s, gb_s, of_s, ob_s, st_s = rest[-8:]
    hg = GLA_HEAD_GROUP
    glr = glr_ref[...].astype(BF16)

    def gate(w_ref, b_ref):
        x = jnp.dot(glr, w_ref[...].astype(BF16), preferred_element_type=F32) + b_ref[...]
        return _log_sigmoid(x) * (1.0 / GATE_TAU)

    gf_s[...] = gate(wgf_ref, bgf_ref)
    gb_s[...] = gate(wgb_ref, bgb_ref)
    for j in range(hg):
        st_s[j] = s0f_ref[j].T
        st_s[hg + j] = s0b_ref[j].T

    row = lax.broadcasted_iota(jnp.int32, (CHUNK, CHUNK), 0)
    col = lax.broadcasted_iota(jnp.int32, (CHUNK, CHUNK), 1)
    nt = (((1,), (1,)), ((), ()))
    tn = (((0,), (0,)), ((), ()))

    keep_f = col <= row
    keep_b = col >= row

    def cumulate(n, carry):
        sl = pl.ds(pl.multiple_of(n * CHUNK, CHUNK), CHUNK)
        gf_s[sl, :] = _exact_tri_dot(keep_f, gf_s[sl, :])
        gb_s[sl, :] = _exact_tri_dot(keep_b, gb_s[sl, :])
        return carry

    lax.fori_loop(0, n_chunks, cumulate, 0)

    def step(n, carry):
        units = []
        for j in range(hg):
            units.append((n, j, j, gf_s, of_s, keep_f, CHUNK - 1, CHUNK // 2 - 1))
            units.append((n_chunks - 1 - n, j, hg + j, gb_s, ob_s, keep_b, 0, CHUNK // 2))
        staged = []
        for cn, j, slot, b_s, o_s, keep, end_idx, mid_idx in units:
            sl = pl.ds(pl.multiple_of(cn * CHUNK, CHUNK), CHUNK)
            dk = slice(j * GLA_DK, (j + 1) * GLA_DK)
            dv = slice(j * GLA_DV, (j + 1) * GLA_DV)
            b = b_s[sl, dk]
            q = q_ref[sl, dk] * GLA_SCALE
            k = k_ref[sl, dk]
            v = v_ref[sl, dv].astype(BF16)
            b_end = b[end_idx:end_idx + 1, :]
            b_mid = b[mid_idx:mid_idx + 1, :]
            qe = (q * jnp.exp(b)).astype(BF16)
            qa = (q * jnp.exp(b - b_mid)).astype(BF16)
            ka = (k * jnp.exp(b_mid - b)).astype(BF16)
            kd = (k * jnp.exp(b_end - b)).astype(BF16)
            staged.append((sl, dv, slot, o_s, keep, v, qe, qa, ka, kd, jnp.exp(b_end)))
        scores = [lax.dot_general(qa, ka, nt, preferred_element_type=F32)
                  for _, _, _, _, _, _, _, qa, ka, _, _ in staged]
        inter = [lax.dot_general(qe, st_s[slot].astype(BF16), nt, preferred_element_type=F32)
                 for _, _, slot, _, _, _, qe, _, _, _, _ in staged]
        update = [lax.dot_general(v, kd, tn, preferred_element_type=F32)
                  for _, _, _, _, _, v, _, _, _, kd, _ in staged]
        for (sl, dv, slot, o_s, keep, v, _, _, _, _, decay), a, o_inter, upd in zip(staged, scores, inter, update):
            a = jnp.where(keep, a, 0.0).astype(BF16)
            o_s[sl, dv] = jnp.dot(a, v, preferred_element_type=F32) + o_inter
            st_s[slot] = st_s[slot] * decay + upd
        return carry

    lax.fori_loop(0, n_chunks, step, 0)
    for j in range(hg):
        sf_ref[j] = st_s[j].T
        sb_ref[j] = st_s[hg + j].T
        dv = slice(j * GLA_DV, (j + 1) * GLA_DV)
        og = og_ref[:, dv]
        o = _rms(of_s[:, dv] + ob_s[:, dv], gn_ref[...], GLA_DV)
        o_ref[:, dv] = (o * (og / (1.0 + jnp.exp(-og)))).astype(BF16)


def _gla(proj, wg, bg, s0f, s0b, gn, mix, *, n_batch, t, row0):
    hg = GLA_HEAD_GROUP
    rb0 = row0 // t
    n_groups = GLA_HEADS // hg
    tok = lambda width, col0: pl.BlockSpec(
        (t, hg * width), lambda b, h, cb=col0 // (hg * width): (rb0 + b, cb + h))
    state = pl.BlockSpec((None, hg, GLA_DK, GLA_DV), lambda b, h: (b, h, 0, 0))
    in_specs = [
        tok(GLA_DK, P_GQ),
        tok(GLA_DK, P_GK),
        tok(GLA_DV, P_GV),
        pl.BlockSpec((t, V7X_LANES), lambda b, h: (rb0 + b, P_GATE // V7X_LANES)),
        pl.BlockSpec((V7X_LANES, hg * GLA_DK), lambda b, h: (0, h)),
        pl.BlockSpec((V7X_LANES, hg * GLA_DK), lambda b, h: (0, n_groups + h)),
        pl.BlockSpec((1, hg * GLA_DK), lambda b, h: (0, h)),
        pl.BlockSpec((1, hg * GLA_DK), lambda b, h: (0, n_groups + h)),
        state, state,
        tok(GLA_DV, P_OG),
        pl.BlockSpec((1, GLA_DV), lambda b, h: (0, 0)),
    ]
    args = [proj, proj, proj, proj, wg, wg, bg, bg, s0f, s0b, proj, gn]
    aliases = _mix_alias(mix, args, in_specs)
    return pl.pallas_call(
        functools.partial(_gla_kernel, n_chunks=t // CHUNK),
        out_shape=(jax.ShapeDtypeStruct((N_TOK, MLA_OUT + GLA_OUT), BF16),
                   jax.ShapeDtypeStruct((n_batch, GLA_HEADS, GLA_DK, GLA_DV), F32),
                   jax.ShapeDtypeStruct((n_batch, GLA_HEADS, GLA_DK, GLA_DV), F32)),
        grid=(n_batch, n_groups),
        in_specs=in_specs,
        input_output_aliases=aliases,
        out_specs=(pl.BlockSpec((t, hg * GLA_DV),
                                lambda b, h: (rb0 + b, MLA_OUT // (hg * GLA_DV) + h)), state, state),
        scratch_shapes=[pltpu.VMEM((t, hg * GLA_DK), F32), pltpu.VMEM((t, hg * GLA_DK), F32),
                        pltpu.VMEM((t, hg * GLA_DV), F32), pltpu.VMEM((t, hg * GLA_DV), F32),
                        pltpu.VMEM((2 * hg, GLA_DV, GLA_DK), F32)],
        compiler_params=_cparams(
            ("parallel", "parallel"),
            2 * (t * hg * (2 * GLA_DK + 2 * GLA_DV) * 4 + t * V7X_LANES * 4 + t * hg * GLA_DV * 2)
            + t * hg * (2 * GLA_DK + 2 * GLA_DV) * 4 + 6 * t * hg * GLA_DK * 4
            + 10 * hg * GLA_DK * GLA_DV * 4 + 4 * V7X_LANES * hg * GLA_DK * 2),
        name="gla_%d" % t,
    )(*args)


def _pack_w_in_kernel(w_ref, o_ref):
    segments = ((0, 0, OFF_KV + KV_LORA), (P_KROPE, OFF_KV + KV_LORA, QK_ROPE), (P_GATE, OFF_GATE, 2 * GATE_RANK),
                (P_GQ, OFF_GQ, OFF_GATE - OFF_GQ), (P_OG, OFF_OG, GLA_OUT))
    rows = o_ref.shape[0]
    o_ref[:, P_KROPE:P_GQ] = jnp.zeros((rows, P_GQ - P_KROPE), BF16)
    for dst, src, width in segments:
        o_ref[:, dst:dst + width] = w_ref[:, src:src + width].astype(BF16)


def _pack_w_in(w_in, layer):
    tr = 256
    return pl.pallas_call(
        _pack_w_in_kernel,
        out_shape=jax.ShapeDtypeStruct((D_MODEL, P_TOTAL), BF16),
        grid=(D_MODEL // tr,),
        in_specs=[pl.BlockSpec((None, tr, N_IN), lambda i: (layer, i, 0))],
        out_specs=pl.BlockSpec((tr, P_TOTAL), lambda i: (i, 0)),
        compiler_params=_cparams(("parallel",), 2 * tr * (N_IN * 4 + P_TOTAL * 2) + 2 * tr * P_TOTAL * 4),
        name="pack_w_in",
    )(w_in)


def _pack_w_qb(w):
    w = w.reshape(Q_LORA, MLA_HEADS, QK_NOPE + QK_ROPE)
    w = jnp.pad(w, ((0, 0), (0, 0), (0, QK_PAD - QK_NOPE - QK_ROPE)))
    return w.reshape(Q_LORA, MLA_HEADS * QK_PAD).astype(BF16)


def _pack_w_kvb(w):
    w = w.reshape(KV_LORA, MLA_HEADS, QK_NOPE + V_HEAD)
    return jnp.concatenate([w[:, :, :QK_NOPE].reshape(KV_LORA, MLA_OUT),
                            w[:, :, QK_NOPE:].reshape(KV_LORA, MLA_OUT)], axis=1).astype(BF16)


def _pack_gate(w_gf2, w_gb2, b_gf, b_gb):
    wg = jnp.zeros((V7X_LANES, 2 * GLA_QK), F32)
    wg = wg.at[:GATE_RANK, :GLA_QK].set(w_gf2).at[GATE_RANK:2 * GATE_RANK, GLA_QK:].set(w_gb2)
    return wg.astype(BF16), jnp.concatenate([b_gf, b_gb]).reshape(1, 2 * GLA_QK)


def _pad_lanes(g):
    return jnp.pad(g, (0, V7X_LANES - g.shape[0])).reshape(1, V7X_LANES)


def _rope_tables():
    rows = DEC_SEQ // GRID_W
    row = jnp.repeat(jnp.arange(rows, dtype=F32), GRID_W)
    col = jnp.tile(jnp.arange(GRID_W, dtype=F32), rows)
    n_freq = QK_ROPE // 4
    inv = jnp.power(ROPE_BASE, -jnp.arange(n_freq, dtype=F32) / n_freq)
    ang = jnp.concatenate([row[:, None] * inv, col[:, None] * inv], axis=-1)
    cos, sin = jnp.cos(ang), jnp.sin(ang)
    half = QK_ROPE // 2
    zpad = jnp.zeros((DEC_SEQ, V7X_LANES - QK_ROPE), F32)
    zhalf = jnp.zeros((DEC_SEQ, half), F32)
    cos_t = jnp.concatenate([cos, cos, zpad], axis=1)
    sa_t = jnp.concatenate([-sin, zhalf, zpad], axis=1)
    sb_t = jnp.concatenate([zhalf, sin, zpad], axis=1)
    ident = jnp.concatenate([jnp.ones((DEC_SEQ, QK_ROPE), F32), zpad], axis=1)
    zeros = jnp.zeros((DEC_SEQ, V7X_LANES), F32)
    return (jnp.stack([ident, cos_t]), jnp.stack([zeros, sa_t]), jnp.stack([zeros, sb_t]))


def kernel(x_prompt, x_sample, c, cache_ckv, cache_krope, state_gla_fwd, state_gla_bwd, c_ctx, w_ada, b_ada, norm1, norm2, w_in, q_a_norm, w_qb, kv_a_norm, w_kvb, q_norm_nope, q_norm_rope, k_norm_nope, k_norm_rope, w_gf2, b_gf, w_gb2, b_gb, gla_norm, mla_out_norm, w_o, w_up, w_down):
    x = (x_prompt.reshape(N_PROMPT, D_MODEL), x_sample.reshape(N_SAMPLE, D_MODEL))
    cvec = jnp.concatenate([c_ctx[None, :], c, jnp.zeros((N_MOD_ROWS - 1 - DEC_BATCH, D_MODEL), F32)], axis=0)
    mods = _ada(cvec, w_ada, b_ada).reshape(DEPTH, N_MOD_ROWS, 6, D_MODEL)
    rope_tabs = _rope_tables()
    zero_state = jnp.zeros((BATCH, GLA_HEADS, GLA_DK, GLA_DV), F32)

    ckvs, kropes, sfs, sbs = [], [], [], []
    h = _normmod(x, norm1[0], mods[0], 0, 1)
    for l in range(DEPTH):
        mod = mods[l]
        proj = _matmul_ws(h, _pack_w_in(w_in, l), None, tm=1024, tn=1024, out_dtype=F32, name="in_proj")

        q = _q_proj(proj, q_a_norm[l].reshape(1, Q_LORA), _pack_w_qb(w_qb[l]),
                    q_norm_nope[l].reshape(1, QK_NOPE), _pad_lanes(q_norm_rope[l]), rope_tabs)
        ckv, kr, krr = _kv_norm(proj, kv_a_norm[l].reshape(1, KV_LORA), _pad_lanes(k_norm_rope[l]), rope_tabs)
        ckv_all = jnp.concatenate([ckv, cache_ckv[:, l].reshape(N_CACHE, KV_LORA)], axis=0)
        kv = _kv_up(ckv_all, _pack_w_kvb(w_kvb[l]), k_norm_nope[l].reshape(1, QK_NOPE))
        kr_cache = jnp.pad(cache_krope[:, l].reshape(N_CACHE, QK_ROPE),
                           ((0, 0), (0, V7X_LANES - QK_ROPE))).astype(BF16)
        kr_all = jnp.concatenate([krr, kr_cache], axis=0)
        g_mla = mla_out_norm[l].reshape(1, MLA_OUT)
        mix = _attention(q, kv, kr_all, g_mla, None, n_batch=BATCH, t_q=SEQ, q_row0=0,
                         segs=[(SEQ, 0)])
        mix = _attention(q, kv, kr_all, g_mla, mix, n_batch=DEC_BATCH, t_q=DEC_SEQ, q_row0=N_PROMPT,
                         segs=[(DEC_SEQ, N_PROMPT), (PAST_LEN, N_TOK)])

        wg, bg = _pack_gate(w_gf2[l], w_gb2[l], b_gf[l], b_gb[l])
        gn = gla_norm[l].reshape(1, GLA_DV)
        mix, s_f, s_b = _gla(proj, wg, bg, zero_state, zero_state, gn, mix, n_batch=BATCH, t=SEQ, row0=0)
        mix, _, _ = _gla(proj, wg, bg, state_gla_fwd[:, l], state_gla_bwd[:, l], gn, mix,
                         n_batch=DEC_BATCH, t=DEC_SEQ, row0=N_PROMPT)

        x = _matmul_resid(mix, w_o, l, x, mod, 2, name="out_proj")

        h2 = _normmod(x, norm2[l], mod, 3, 4)
        act = _matmul_ws(h2, w_up, l, tm=512, tn=1024, out_dtype=BF16, relu2=True, name="mlp_up")
        x = _matmul_resid(act, w_down, l, x, mod, 5, name="mlp_down")
        if l + 1 < DEPTH:
            h = _normmod(x, norm1[l + 1], mods[l + 1], 0, 1)

        ckvs.append(ckv[:N_PROMPT].reshape(BATCH, SEQ, KV_LORA))
        kropes.append(kr[:N_PROMPT, :QK_ROPE].reshape(BATCH, SEQ, QK_ROPE))
        sfs.append(s_f)
        sbs.append(s_b)

    return (x[:N_PROMPT].reshape(BATCH, SEQ, D_MODEL),
            x[N_PROMPT:].reshape(DEC_BATCH, DEC_SEQ, D_MODEL),
            jnp.stack(ckvs, axis=1), jnp.stack(kropes, axis=1),
            jnp.stack(sfs, axis=1), jnp.stack(sbs, axis=1))
```

```python
import functools

import jax
import jax.numpy as jnp
from jax import lax
from jax.experimental import pallas as pl
from jax.experimental.pallas import tpu as pltpu

D_MODEL = 4096
BATCH = 16
SEQ = 256
DEPTH = 2
DEC_BATCH = 4
DEC_SEQ = 1024
PAST_LEN = 512
GRID_W = 64
MLA_HEADS = 16
QK_NOPE = 128
QK_ROPE = 64
V_HEAD = 128
Q_LORA = 1024
KV_LORA = 512
GLA_HEADS = 8
GLA_DK = 128
GLA_DV = 256
GATE_RANK = 16
GATE_TAU = 16.0
CHUNK = 64
D_FF = 4 * D_MODEL
ROPE_BASE = 10000.0
EPS = 1e-6

MLA_OUT = MLA_HEADS * V_HEAD
GLA_QK = GLA_HEADS * GLA_DK
GLA_OUT = GLA_HEADS * GLA_DV
OFF_KV = Q_LORA
OFF_GQ = OFF_KV + KV_LORA + QK_ROPE
OFF_GK = OFF_GQ + GLA_QK
OFF_GV = OFF_GK + GLA_QK
OFF_GATE = OFF_GV + GLA_OUT
OFF_OG = OFF_GATE + 2 * GATE_RANK
N_IN = OFF_OG + GLA_OUT

F32 = jnp.float32
BF16 = jnp.bfloat16

V7X_LANES = 128
V7X_SCOPED_VMEM_BYTES = 60000 * 1024

N_PROMPT = BATCH * SEQ
N_SAMPLE = DEC_BATCH * DEC_SEQ
N_TOK = N_PROMPT + N_SAMPLE
N_CACHE = DEC_BATCH * PAST_LEN
N_MOD_ROWS = 8

P_KV = 1024
P_KROPE = P_KV + KV_LORA
P_GATE = P_KROPE + V7X_LANES
P_GQ = 2048
P_GK = P_GQ + GLA_QK
P_GV = P_GK + GLA_QK
P_OG = P_GV + GLA_OUT
P_TOTAL = P_OG + GLA_OUT
QK_PAD = 2 * V7X_LANES

ATTN_SCALE = float((QK_NOPE + QK_ROPE) ** -0.5)
GLA_SCALE = float(GLA_DK ** -0.5)


def _cparams(sem, vmem_bytes):
    return pltpu.CompilerParams(dimension_semantics=sem,
                                vmem_limit_bytes=min(int(vmem_bytes), V7X_SCOPED_VMEM_BYTES))


def _mod_row(row_block, rows_per_block):
    return jnp.maximum(row_block * rows_per_block // DEC_SEQ - (N_PROMPT // DEC_SEQ - 1), 0)


def _rms(x, g, n):
    ms = jnp.sum(x * x, axis=-1, keepdims=True) * (1.0 / n)
    return x * lax.rsqrt(ms + EPS) * g


def _ada_kernel(c_ref, w_ref, b_ref, o_ref):
    k = pl.program_id(2)

    @pl.when(k == 0)
    def _():
        o_ref[...] = jnp.broadcast_to(b_ref[...], o_ref.shape)

    c = c_ref[...]
    s = (c / (1.0 + jnp.exp(-c))).astype(BF16)
    o_ref[...] += jnp.dot(s, w_ref[...].astype(BF16), preferred_element_type=F32)


def _ada(cvec, w_ada, b_ada):
    tk, tn = 1024, 2048
    n_out = 6 * D_MODEL
    return pl.pallas_call(
        _ada_kernel,
        out_shape=jax.ShapeDtypeStruct((DEPTH, N_MOD_ROWS, n_out), F32),
        grid=(DEPTH, n_out // tn, D_MODEL // tk),
        in_specs=[
            pl.BlockSpec((N_MOD_ROWS, tk), lambda l, j, k: (0, k)),
            pl.BlockSpec((None, tk, tn), lambda l, j, k: (l, k, j)),
            pl.BlockSpec((None, 1, tn), lambda l, j, k: (l, 0, j)),
        ],
        out_specs=pl.BlockSpec((None, N_MOD_ROWS, tn), lambda l, j, k: (l, 0, j)),
        compiler_params=_cparams(("parallel", "parallel", "arbitrary"), 3 * tk * tn * 4),
        name="ada_mod",
    )(cvec, w_ada, b_ada.reshape(DEPTH, 1, n_out))


ROW_TILE = 512


def _stream_specs(x, tm=ROW_TILE, tn=D_MODEL):
    col = (lambda idx: idx[1]) if tn != D_MODEL else (lambda idx: 0)
    if not isinstance(x, tuple):
        return [pl.BlockSpec((tm, tn), lambda *idx: (idx[0], col(idx)))], [x]
    pb = N_PROMPT // tm
    last_col = D_MODEL // tn - 1
    return ([pl.BlockSpec((tm, tn), lambda *idx: (jnp.minimum(idx[0], pb - 1),
                                                  jnp.where(idx[0] < pb, col(idx), last_col))),
             pl.BlockSpec((tm, tn), lambda *idx: (jnp.maximum(idx[0] - pb, 0),
                                                  jnp.where(idx[0] < pb, 0, col(idx))))], list(x))


def _load_stream(x_refs, tm=ROW_TILE):
    if len(x_refs) == 1:
        return x_refs[0][...]
    in_prompt = pl.program_id(0) < N_PROMPT // tm
    return jnp.where(in_prompt, x_refs[0][...], x_refs[1][...])


def _modulated_norm(x, g_ref, mod_ref, shift_idx, scale_idx):
    y = _rms(x, g_ref[...], D_MODEL)
    sc = mod_ref[0, scale_idx:scale_idx + 1, :]
    sh = mod_ref[0, shift_idx:shift_idx + 1, :]
    return (y * (1.0 + sc) + sh).astype(BF16)


def _normmod_kernel(*refs, n_x, shift_idx, scale_idx):
    g_ref, mod_ref, o_ref = refs[n_x:]
    o_ref[...] = _modulated_norm(_load_stream(refs[:n_x]), g_ref, mod_ref, shift_idx, scale_idx)


def _normmod(x, g, mod, shift_idx, scale_idx):
    x_specs, x_args = _stream_specs(x)
    row = pl.BlockSpec((ROW_TILE, D_MODEL), lambda i: (i, 0))
    return pl.pallas_call(
        functools.partial(_normmod_kernel, n_x=len(x_args), shift_idx=shift_idx, scale_idx=scale_idx),
        out_shape=jax.ShapeDtypeStruct((N_TOK, D_MODEL), BF16),
        grid=(N_TOK // ROW_TILE,),
        in_specs=x_specs + [
            pl.BlockSpec((1, D_MODEL), lambda i: (0, 0)),
            pl.BlockSpec((1, 6, D_MODEL), lambda i: (_mod_row(i, ROW_TILE), 0, 0)),
        ],
        out_specs=row,
        compiler_params=_cparams(("arbitrary",), 8 * ROW_TILE * D_MODEL * 4),
        name="norm_mod",
    )(*x_args, g.reshape(1, D_MODEL), mod)


def _mm_resid_kernel(a_ref, b_ref, *rest, nk, n_x, tm, gate_idx):
    x_refs, mod_refs, o_ref = rest[:n_x], rest[n_x:-1], rest[-1]
    k = pl.program_id(2)

    def prod():
        return jnp.dot(a_ref[...], b_ref[...].astype(BF16), preferred_element_type=F32)

    @pl.when(k == 0)
    def _():
        o_ref[...] = prod()

    @pl.when(jnp.logical_and(k > 0, k < nk - 1))
    def _():
        o_ref[...] += prod()

    @pl.when(k == nk - 1)
    def _():
        upd = o_ref[...] + prod()
        x = _load_stream(x_refs, tm)
        rows = tm // len(mod_refs)
        for part, mod_ref in enumerate(mod_refs):
            sl = slice(part * rows, (part + 1) * rows)
            o_ref[sl, :] = x[sl, :] + mod_ref[0, gate_idx:gate_idx + 1, :] * upd[sl, :]


def _matmul_resid(a, b, layer, x, mod, gate_idx, *, name):
    tm, tn, tk = (1024 if isinstance(x, tuple) else 2048), 1024, 1024
    m, kdim = a.shape
    n = b.shape[-1]
    nk = kdim // tk
    x_specs, x_args = _stream_specs(x, tm, tn)
    n_mod = tm // DEC_SEQ
    mod_specs = [pl.BlockSpec((1, 6, tn), lambda i, j, k, p=p: (_mod_row(i * n_mod + p, DEC_SEQ), 0, j))
                 for p in range(n_mod)]
    vmem = (2 * (tm * tk * 2 + tk * tn * b.dtype.itemsize + (1 + len(x_args)) * tm * tn * 4)
            + 2 * tk * tn * 2 + 2 * tm * tk * 2)
    return pl.pallas_call(
        functools.partial(_mm_resid_kernel, nk=nk, n_x=len(x_args), tm=tm, gate_idx=gate_idx),
        out_shape=jax.ShapeDtypeStruct((m, n), F32),
        grid=(m // tm, n // tn, nk),
        in_specs=[pl.BlockSpec((tm, tk), lambda i, j, k: (i, k)),
                  pl.BlockSpec((None, tk, tn), lambda i, j, k: (layer, k, j))] + x_specs + mod_specs,
        out_specs=pl.BlockSpec((tm, tn), lambda i, j, k: (i, j)),
        compiler_params=_cparams(("arbitrary", "arbitrary", "arbitrary"), vmem),
        name=name,
    )(a, b, *x_args, *([mod] * n_mod))


def _mm_ws_kernel(a_ref, b_ref, o_ref, *scratch, relu2, b_transposed):
    if scratch:
        bw_ref = scratch[0]

        @pl.when(pl.program_id(1) == 0)
        def _():
            bw_ref[...] = b_ref[...].astype(BF16)

        b = bw_ref[...]
    else:
        b = b_ref[...]
    contract_b = 1 if b_transposed else 0
    r = lax.dot_general(a_ref[...], b, (((1,), (contract_b,)), ((), ())), preferred_element_type=F32)
    if relu2:
        r = jnp.square(jnp.maximum(r, 0.0))
    o_ref[...] = r.astype(o_ref.dtype)


def _matmul_ws(a, b, layer, *, tm, tn, out_dtype, relu2=False, b_transposed=False, name):
    m, kdim = a.shape
    if b_transposed:
        n = b.shape[0]
        b_spec = pl.BlockSpec((tn, kdim), lambda j, i: (j, 0))
    elif layer is None:
        n = b.shape[-1]
        b_spec = pl.BlockSpec((kdim, tn), lambda j, i: (0, j))
    else:
        n = b.shape[-1]
        b_spec = pl.BlockSpec((None, kdim, tn), lambda j, i: (layer, 0, j))
    cast = b.dtype != jnp.dtype(BF16)
    o_bytes = jnp.dtype(out_dtype).itemsize
    vmem = (2 * (tm * kdim * 2 + kdim * tn * b.dtype.itemsize + tm * tn * o_bytes)
            + cast * kdim * tn * 2 + 2 * tm * tn * 4)
    return pl.pallas_call(
        functools.partial(_mm_ws_kernel, relu2=relu2, b_transposed=b_transposed),
        out_shape=jax.ShapeDtypeStruct((m, n), out_dtype),
        grid=(n // tn, m // tm),
        in_specs=[pl.BlockSpec((tm, kdim), lambda j, i: (i, 0)), b_spec],
        out_specs=pl.BlockSpec((tm, tn), lambda j, i: (i, j)),
        scratch_shapes=[pltpu.VMEM((tn, kdim) if b_transposed else (kdim, tn), BF16)] if cast else [],
        compiler_params=_cparams(("parallel", "arbitrary"), vmem),
        name=name,
    )(a, b)


def _rope128(r, cos, sa, sb):
    return r * cos + pltpu.roll(r, 96, 1) * sa + pltpu.roll(r, 32, 1) * sb


def _q_kernel(ql_ref, ga_ref, w_ref, gn_ref, gr_ref, cos_ref, sa_ref, sb_ref, o_ref, qn_ref, *, heads_per_tile):
    @pl.when(pl.program_id(1) == 0)
    def _():
        qn_ref[...] = _rms(ql_ref[...], ga_ref[...], Q_LORA).astype(BF16)

    cos, sa, sb = cos_ref[...], sa_ref[...], sb_ref[...]
    qn = qn_ref[...]
    for hh in range(heads_per_tile):
        c0 = hh * QK_PAD
        acc = jnp.dot(qn, w_ref[:, c0:c0 + QK_PAD], preferred_element_type=F32)
        nope = acc[:, :QK_NOPE]
        rope = acc[:, QK_NOPE:]
        nope = _rms(nope, gn_ref[...], QK_NOPE) * ATTN_SCALE
        rope = _rope128(_rms(rope, gr_ref[...], QK_ROPE), cos, sa, sb) * ATTN_SCALE
        o_ref[:, c0:c0 + QK_NOPE] = nope.astype(BF16)
        o_ref[:, c0 + QK_NOPE:c0 + QK_PAD] = rope.astype(BF16)


def _rope_specs(tm, nargs):
    blocks_per_seq = DEC_SEQ // tm

    def idx(i, *_):
        return (jnp.minimum(i * tm // N_PROMPT, 1), i % blocks_per_seq, 0)

    return [pl.BlockSpec((None, tm, V7X_LANES), idx) for _ in range(nargs)]


def _q_proj(proj, ga, w_qb_p, gn, gr128, rope_tabs):
    tm, tn = 512, 1024
    n = MLA_HEADS * QK_PAD
    return pl.pallas_call(
        functools.partial(_q_kernel, heads_per_tile=tn // QK_PAD),
        out_shape=jax.ShapeDtypeStruct((N_TOK, n), BF16),
        grid=(N_TOK // tm, n // tn),
        in_specs=[
            pl.BlockSpec((tm, Q_LORA), lambda i, j: (i, 0)),
            pl.BlockSpec((1, Q_LORA), lambda i, j: (0, 0)),
            pl.BlockSpec((Q_LORA, tn), lambda i, j: (0, j)),
            pl.BlockSpec((1, QK_NOPE), lambda i, j: (0, 0)),
            pl.BlockSpec((1, V7X_LANES), lambda i, j: (0, 0)),
        ] + _rope_specs(tm, 3),
        out_specs=pl.BlockSpec((tm, tn), lambda i, j: (i, j)),
        scratch_shapes=[pltpu.VMEM((tm, Q_LORA), BF16)],
        compiler_params=_cparams(
            ("parallel", "arbitrary"),
            2 * (tm * Q_LORA * 4 + Q_LORA * tn * 2 + tm * tn * 2) + tm * Q_LORA * 2 + 3 * tm * tn * 4),
        name="mla_q_proj",
    )(proj, ga, w_qb_p, gn, gr128, *rope_tabs)


def _kvnorm_kernel(c_ref, r_ref, gc_ref, gk_ref, cos_ref, sa_ref, sb_ref, ckv_ref, kr_ref, krr_ref):
    ckv_ref[...] = _rms(c_ref[...], gc_ref[...], KV_LORA)
    rn = _rms(r_ref[...], gk_ref[...], QK_ROPE)
    kr_ref[...] = rn
    krr_ref[...] = _rope128(rn, cos_ref[...], sa_ref[...], sb_ref[...]).astype(BF16)


def _kv_norm(proj, gc, gk128, rope_tabs):
    tm = 512
    return pl.pallas_call(
        _kvnorm_kernel,
        out_shape=(jax.ShapeDtypeStruct((N_TOK, KV_LORA), F32),
                   jax.ShapeDtypeStruct((N_TOK, V7X_LANES), F32),
                   jax.ShapeDtypeStruct((N_TOK, V7X_LANES), BF16)),
        grid=(N_TOK // tm,),
        in_specs=[
            pl.BlockSpec((tm, KV_LORA), lambda i: (i, P_KV // KV_LORA)),
            pl.BlockSpec((tm, V7X_LANES), lambda i: (i, P_KROPE // V7X_LANES)),
            pl.BlockSpec((1, KV_LORA), lambda i: (0, 0)),
            pl.BlockSpec((1, V7X_LANES), lambda i: (0, 0)),
        ] + _rope_specs(tm, 3),
        out_specs=(pl.BlockSpec((tm, KV_LORA), lambda i: (i, 0)),
                   pl.BlockSpec((tm, V7X_LANES), lambda i: (i, 0)),
                   pl.BlockSpec((tm, V7X_LANES), lambda i: (i, 0))),
        compiler_params=_cparams(("parallel",), 8 * tm * (KV_LORA + 4 * V7X_LANES) * 4),
        name="mla_kv_norm",
    )(proj, proj, gc, gk128, *rope_tabs)


def _kvup_kernel(c_ref, w_ref, gn_ref, o_ref, *, heads_per_tile, k_tiles):
    c = c_ref[...].astype(BF16)
    pair = 2 * QK_NOPE

    @pl.when(pl.program_id(1) < k_tiles)
    def _():
        for pp in range(heads_per_tile // 2):
            acc = jnp.dot(c, w_ref[:, pp * pair:(pp + 1) * pair], preferred_element_type=F32)
            for hh in range(2):
                c0 = pp * pair + hh * QK_NOPE
                o_ref[:, c0:c0 + QK_NOPE] = _rms(
                    acc[:, hh * QK_NOPE:(hh + 1) * QK_NOPE], gn_ref[...], QK_NOPE).astype(BF16)

    @pl.when(pl.program_id(1) >= k_tiles)
    def _():
        o_ref[...] = jnp.dot(c, w_ref[...], preferred_element_type=F32).astype(BF16)


def _kv_up(ckv_all, w_kvb_p, gn):
    tm, tn = 512, 1024
    m = ckv_all.shape[0]
    n = 2 * MLA_OUT
    return pl.pallas_call(
        functools.partial(_kvup_kernel, heads_per_tile=tn // QK_NOPE, k_tiles=MLA_OUT // tn),
        out_shape=jax.ShapeDtypeStruct((m, n), BF16),
        grid=(m // tm, n // tn),
        in_specs=[
            pl.BlockSpec((tm, KV_LORA), lambda i, j: (i, 0)),
            pl.BlockSpec((KV_LORA, tn), lambda i, j: (0, j)),
            pl.BlockSpec((1, QK_NOPE), lambda i, j: (0, 0)),
        ],
        out_specs=pl.BlockSpec((tm, tn), lambda i, j: (i, j)),
        compiler_params=_cparams(
            ("parallel", "parallel"),
            2 * (tm * KV_LORA * 4 + KV_LORA * tn * 2 + tm * tn * 2) + 3 * tm * tn * 4),
        name="mla_kv_up",
    )(ckv_all, w_kvb_p, gn)


def _attn_kernel(*refs, n_seg):
    q_ref = refs[0]
    seg_refs = [refs[1 + 3 * s:4 + 3 * s] for s in range(n_seg)]
    g_ref = refs[1 + 3 * n_seg]
    o_ref, oacc_ref = refs[-2:]
    def head_scores(h):
        qh = q_ref[:, h * QK_PAD:(h + 1) * QK_PAD]
        out = []
        for kn_ref, kr_ref, _ in seg_refs:
            kh = jnp.concatenate([kn_ref[:, h * QK_NOPE:(h + 1) * QK_NOPE], kr_ref[...]], axis=1)
            out.append(lax.dot_general(qh, kh, (((1,), (1,)), ((), ())), preferred_element_type=F32))
        return out

    next_scores = head_scores(0)
    for h in range(MLA_HEADS):
        scores = next_scores
        if h + 1 < MLA_HEADS:
            next_scores = head_scores(h + 1)
        m = scores[0].max(axis=-1, keepdims=True)
        for s in scores[1:]:
            m = jnp.maximum(m, s.max(axis=-1, keepdims=True))
        denom = None
        out = None
        for s, (_, _, v_ref) in zip(scores, seg_refs):
            p = jnp.exp(s - m)
            ps = p.sum(axis=-1, keepdims=True)
            pv = jnp.dot(p.astype(BF16), v_ref[:, h * V_HEAD:(h + 1) * V_HEAD], preferred_element_type=F32)
            denom = ps if denom is None else denom + ps
            out = pv if out is None else out + pv
        oacc_ref[:, h * V_HEAD:(h + 1) * V_HEAD] = out / denom
    o_ref[...] = _rms(oacc_ref[...], g_ref[...], MLA_OUT).astype(BF16)


def _mix_alias(mix, args, in_specs):
    if mix is None:
        return {}
    args.append(mix)
    in_specs.append(pl.BlockSpec(memory_space=pl.ANY))
    return {len(args) - 1: 0}


def _attention(q, kv, kr, g_out, mix, *, n_batch, t_q, segs, q_row0):
    tq = 256
    qb = t_q // tq
    in_specs = [pl.BlockSpec((tq, MLA_HEADS * QK_PAD), lambda b, i: (q_row0 // tq + b * qb + i, 0))]
    args = [q]
    vmem = 2 * tq * MLA_HEADS * QK_PAD * 2
    for rows, row0 in segs:
        blk0 = row0 // rows
        in_specs += [
            pl.BlockSpec((rows, MLA_OUT), lambda b, i, blk0=blk0: (blk0 + b, 0)),
            pl.BlockSpec((rows, V7X_LANES), lambda b, i, blk0=blk0: (blk0 + b, 0)),
            pl.BlockSpec((rows, MLA_OUT), lambda b, i, blk0=blk0: (blk0 + b, 1)),
        ]
        args += [kv, kr, kv]
        vmem += 2 * rows * (2 * MLA_OUT + V7X_LANES) * 2 + 12 * tq * rows * 4
    in_specs.append(pl.BlockSpec((1, MLA_OUT), lambda b, i: (0, 0)))
    args.append(g_out)
    vmem += 4 * tq * MLA_OUT * 4
    aliases = _mix_alias(mix, args, in_specs)
    return pl.pallas_call(
        functools.partial(_attn_kernel, n_seg=len(segs)),
        out_shape=jax.ShapeDtypeStruct((N_TOK, MLA_OUT + GLA_OUT), BF16),
        grid=(n_batch, qb),
        in_specs=in_specs,
        out_specs=pl.BlockSpec((tq, MLA_OUT), lambda b, i: (q_row0 // tq + b * qb + i, 0)),
        scratch_shapes=[pltpu.VMEM((tq, MLA_OUT), F32)],
        input_output_aliases=aliases,
        compiler_params=_cparams(("parallel", "arbitrary"), vmem),
        name="mla_attention_%d" % t_q,
    )(*args)


def _log_sigmoid(x):
    return -(jnp.maximum(-x, 0.0) + jnp.log(1.0 + jnp.exp(-jnp.abs(x))))


def _exact_tri_dot(keep, g):
    tri = jnp.where(keep, 1.0, 0.0).astype(BF16)
    g_hi = g.astype(BF16)
    rem = g - g_hi.astype(F32)
    g_mid = rem.astype(BF16)
    g_lo = (rem - g_mid.astype(F32)).astype(BF16)
    return (jnp.dot(tri, g_hi, preferred_element_type=F32)
            + jnp.dot(tri, g_mid, preferred_element_type=F32)
            + jnp.dot(tri, g_lo, preferred_element_type=F32))


GLA_HEAD_GROUP = 4


def _gla_kernel(q_ref, k_ref, v_ref, glr_ref, wgf_ref, wgb_ref, bgf_ref, bgb_ref, s0f_ref, s0b_ref,
                og_ref, gn_ref, *rest, n_chunks):
    o_ref, sf_ref, sb_ref, gf_s, gb_s, of_s, ob_s, st_s = rest[-8:]
    hg = GLA_HEAD_GROUP
    glr = glr_ref[...].astype(BF16)

    def gate(w_ref, b_ref):
        x = jnp.dot(glr, w_ref[...].astype(BF16), preferred_element_type=F32) + b_ref[...]
        return _log_sigmoid(x) * (1.0 / GATE_TAU)

    gf_s[...] = gate(wgf_ref, bgf_ref)
    gb_s[...] = gate(wgb_ref, bgb_ref)
    for j in range(hg):
        st_s[j] = s0f_ref[j].T
        st_s[hg + j] = s0b_ref[j].T

    row = lax.broadcasted_iota(jnp.int32, (CHUNK, CHUNK), 0)
    col = lax.broadcasted_iota(jnp.int32, (CHUNK, CHUNK), 1)
    nt = (((1,), (1,)), ((), ()))
    tn = (((0,), (0,)), ((), ()))

    keep_f = col <= row
    keep_b = col >= row

    def cumulate(n, carry):
        sl = pl.ds(pl.multiple_of(n * CHUNK, CHUNK), CHUNK)
        gf_s[sl, :] = _exact_tri_dot(keep_f, gf_s[sl, :])
        gb_s[sl, :] = _exact_tri_dot(keep_b, gb_s[sl, :])
        return carry

    lax.fori_loop(0, n_chunks, cumulate, 0)

    def step(n, carry):
        units = []
        for j in range(hg):
            units.append((n, j, j, gf_s, of_s, keep_f, CHUNK - 1, CHUNK // 2 - 1))
            units.append((n_chunks - 1 - n, j, hg + j, gb_s, ob_s, keep_b, 0, CHUNK // 2))
        staged = []
        for cn, j, slot, b_s, o_s, keep, end_idx, mid_idx in units:
            sl = pl.ds(pl.multiple_of(cn * CHUNK, CHUNK), CHUNK)
            dk = slice(j * GLA_DK, (j + 1) * GLA_DK)
            dv = slice(j * GLA_DV, (j + 1) * GLA_DV)
            b = b_s[sl, dk]
            q = q_ref[sl, dk] * GLA_SCALE
            k = k_ref[sl, dk]
            v = v_ref[sl, dv].astype(BF16)
            b_end = b[end_idx:end_idx + 1, :]
            b_mid = b[mid_idx:mid_idx + 1, :]
            qe = (q * jnp.exp(b)).astype(BF16)
            qa = (q * jnp.exp(b - b_mid)).astype(BF16)
            ka = (k * jnp.exp(b_mid - b)).astype(BF16)
            kd = (k * jnp.exp(b_end - b)).astype(BF16)
            staged.append((sl, dv, slot, o_s, keep, v, qe, qa, ka, kd, jnp.exp(b_end)))
        scores = [lax.dot_general(qa, ka, nt, preferred_element_type=F32)
                  for _, _, _, _, _, _, _, qa, ka, _, _ in staged]
        inter = [lax.dot_general(qe, st_s[slot].astype(BF16), nt, preferred_element_type=F32)
                 for _, _, slot, _, _, _, qe, _, _, _, _ in staged]
        update = [lax.dot_general(v, kd, tn, preferred_element_type=F32)
                  for _, _, _, _, _, v, _, _, _, kd, _ in staged]
        for (sl, dv, slot, o_s, keep, v, _, _, _, _, decay), a, o_inter, upd in zip(staged, scores, inter, update):
            a = jnp.where(keep, a, 0.0).astype(BF16)
            o_s[sl, dv] = jnp.dot(a, v, preferred_element_type=F32) + o_inter
            st_s[slot] = st_s[slot] * decay + upd
        return carry

    lax.fori_loop(0, n_chunks, step, 0)
    for j in range(hg):
        sf_ref[j] = st_s[j].T
        sb_ref[j] = st_s[hg + j].T
        dv = slice(j * GLA_DV, (j + 1) * GLA_DV)
        og = og_ref[:, dv]
        o = _rms(of_s[:, dv] + ob_s[:, dv], gn_ref[...], GLA_DV)
        o_ref[:, dv] = (o * (og / (1.0 + jnp.exp(-og)))).astype(BF16)


def _gla(proj, wg, bg, s0f, s0b, gn, mix, *, n_batch, t, row0):
    hg = GLA_HEAD_GROUP
    rb0 = row0 // t
    n_groups = GLA_HEADS // hg
    tok = lambda width, col0: pl.BlockSpec(
        (t, hg * width), lambda b, h, cb=col0 // (hg * width): (rb0 + b, cb + h))
    state = pl.BlockSpec((None, hg, GLA_DK, GLA_DV), lambda b, h: (b, h, 0, 0))
    in_specs = [
        tok(GLA_DK, P_GQ),
        tok(GLA_DK, P_GK),
        tok(GLA_DV, P_GV),
        pl.BlockSpec((t, V7X_LANES), lambda b, h: (rb0 + b, P_GATE // V7X_LANES)),
        pl.BlockSpec((V7X_LANES, hg * GLA_DK), lambda b, h: (0, h)),
        pl.BlockSpec((V7X_LANES, hg * GLA_DK), lambda b, h: (0, n_groups + h)),
        pl.BlockSpec((1, hg * GLA_DK), lambda b, h: (0, h)),
        pl.BlockSpec((1, hg * GLA_DK), lambda b, h: (0, n_groups + h)),
        state, state,
        tok(GLA_DV, P_OG),
        pl.BlockSpec((1, GLA_DV), lambda b, h: (0, 0)),
    ]
    args = [proj, proj, proj, proj, wg, wg, bg, bg, s0f, s0b, proj, gn]
    aliases = _mix_alias(mix, args, in_specs)
    return pl.pallas_call(
        functools.partial(_gla_kernel, n_chunks=t // CHUNK),
        out_shape=(jax.ShapeDtypeStruct((N_TOK, MLA_OUT + GLA_OUT), BF16),
                   jax.ShapeDtypeStruct((n_batch, GLA_HEADS, GLA_DK, GLA_DV), F32),
                   jax.ShapeDtypeStruct((n_batch, GLA_HEADS, GLA_DK, GLA_DV), F32)),
        grid=(n_batch, n_groups),
        in_specs=in_specs,
        input_output_aliases=aliases,
        out_specs=(pl.BlockSpec((t, hg * GLA_DV),
                                lambda b, h: (rb0 + b, MLA_OUT // (hg * GLA_DV) + h)), state, state),
        scratch_shapes=[pltpu.VMEM((t, hg * GLA_DK), F32), pltpu.VMEM((t, hg * GLA_DK), F32),
                        pltpu.VMEM((t, hg * GLA_DV), F32), pltpu.VMEM((t, hg * GLA_DV), F32),
                        pltpu.VMEM((2 * hg, GLA_DV, GLA_DK), F32)],
        compiler_params=_cparams(
            ("parallel", "parallel"),
            2 * (t * hg * (2 * GLA_DK + 2 * GLA_DV) * 4 + t * V7X_LANES * 4 + t * hg * GLA_DV * 2)
            + t * hg * (2 * GLA_DK + 2 * GLA_DV) * 4 + 6 * t * hg * GLA_DK * 4
            + 10 * hg * GLA_DK * GLA_DV * 4 + 4 * V7X_LANES * hg * GLA_DK * 2),
        name="gla_%d" % t,
    )(*args)


def _pack_w_in_kernel(w_ref, o_ref):
    segments = ((0, 0, OFF_KV + KV_LORA), (P_KROPE, OFF_KV + KV_LORA, QK_ROPE), (P_GATE, OFF_GATE, 2 * GATE_RANK),
                (P_GQ, OFF_GQ, OFF_GATE - OFF_GQ), (P_OG, OFF_OG, GLA_OUT))
    cols = o_ref.shape[1]
    o_ref[P_KROPE:P_GQ, :] = jnp.zeros((P_GQ - P_KROPE, cols), BF16)
    for dst, src, rows in segments:
        o_ref[dst:dst + rows, :] = w_ref[src:src + rows, :].astype(BF16)


def _pack_w_in(w_in_t, layer):
    tc = 256
    return pl.pallas_call(
        _pack_w_in_kernel,
        out_shape=jax.ShapeDtypeStruct((P_TOTAL, D_MODEL), BF16),
        grid=(D_MODEL // tc,),
        in_specs=[pl.BlockSpec((None, N_IN, tc), lambda i: (layer, 0, i))],
        out_specs=pl.BlockSpec((P_TOTAL, tc), lambda i: (0, i)),
        compiler_params=_cparams(("parallel",), 2 * tc * (N_IN * 4 + P_TOTAL * 2) + 2 * tc * P_TOTAL * 4),
        name="pack_w_in",
    )(w_in_t)


def _pack_w_qb(w):
    w = w.reshape(Q_LORA, MLA_HEADS, QK_NOPE + QK_ROPE)
    w = jnp.pad(w, ((0, 0), (0, 0), (0, QK_PAD - QK_NOPE - QK_ROPE)))
    return w.reshape(Q_LORA, MLA_HEADS * QK_PAD).astype(BF16)


def _pack_w_kvb(w):
    w = w.reshape(KV_LORA, MLA_HEADS, QK_NOPE + V_HEAD)
    return jnp.concatenate([w[:, :, :QK_NOPE].reshape(KV_LORA, MLA_OUT),
                            w[:, :, QK_NOPE:].reshape(KV_LORA, MLA_OUT)], axis=1).astype(BF16)


def _pack_gate(w_gf2, w_gb2, b_gf, b_gb):
    wg = jnp.zeros((V7X_LANES, 2 * GLA_QK), F32)
    wg = wg.at[:GATE_RANK, :GLA_QK].set(w_gf2).at[GATE_RANK:2 * GATE_RANK, GLA_QK:].set(w_gb2)
    return wg.astype(BF16), jnp.concatenate([b_gf, b_gb]).reshape(1, 2 * GLA_QK)


def _pad_lanes(g):
    return jnp.pad(g, (0, V7X_LANES - g.shape[0])).reshape(1, V7X_LANES)


def _rope_tables():
    rows = DEC_SEQ // GRID_W
    row = jnp.repeat(jnp.arange(rows, dtype=F32), GRID_W)
    col = jnp.tile(jnp.arange(GRID_W, dtype=F32), rows)
    n_freq = QK_ROPE // 4
    inv = jnp.power(ROPE_BASE, -jnp.arange(n_freq, dtype=F32) / n_freq)
    ang = jnp.concatenate([row[:, None] * inv, col[:, None] * inv], axis=-1)
    cos, sin = jnp.cos(ang), jnp.sin(ang)
    half = QK_ROPE // 2
    zpad = jnp.zeros((DEC_SEQ, V7X_LANES - QK_ROPE), F32)
    zhalf = jnp.zeros((DEC_SEQ, half), F32)
    cos_t = jnp.concatenate([cos, cos, zpad], axis=1)
    sa_t = jnp.concatenate([-sin, zhalf, zpad], axis=1)
    sb_t = jnp.concatenate([zhalf, sin, zpad], axis=1)
    ident = jnp.concatenate([jnp.ones((DEC_SEQ, QK_ROPE), F32), zpad], axis=1)
    zeros = jnp.zeros((DEC_SEQ, V7X_LANES), F32)
    return (jnp.stack([ident, cos_t]), jnp.stack([zeros, sa_t]), jnp.stack([zeros, sb_t]))


def kernel(x_prompt, x_sample, c, cache_ckv, cache_krope, state_gla_fwd, state_gla_bwd, c_ctx, w_ada, b_ada, norm1, norm2, w_in, q_a_norm, w_qb, kv_a_norm, w_kvb, q_norm_nope, q_norm_rope, k_norm_nope, k_norm_rope, w_gf2, b_gf, w_gb2, b_gb, gla_norm, mla_out_norm, w_o, w_up, w_down):
    x = (x_prompt.reshape(N_PROMPT, D_MODEL), x_sample.reshape(N_SAMPLE, D_MODEL))
    cvec = jnp.concatenate([c_ctx[None, :], c, jnp.zeros((N_MOD_ROWS - 1 - DEC_BATCH, D_MODEL), F32)], axis=0)
    mods = _ada(cvec, w_ada, b_ada).reshape(DEPTH, N_MOD_ROWS, 6, D_MODEL)
    rope_tabs = _rope_tables()
    zero_state = jnp.zeros((BATCH, GLA_HEADS, GLA_DK, GLA_DV), F32)
    w_in_t = jnp.swapaxes(w_in, 1, 2)

    ckvs, kropes, sfs, sbs = [], [], [], []
    h = _normmod(x, norm1[0], mods[0], 0, 1)
    for l in range(DEPTH):
        mod = mods[l]
        proj = _matmul_ws(h, _pack_w_in(w_in_t, l), None, tm=1024, tn=1024, out_dtype=F32,
                          b_transposed=True, name="in_proj")

        q = _q_proj(proj, q_a_norm[l].reshape(1, Q_LORA), _pack_w_qb(w_qb[l]),
                    q_norm_nope[l].reshape(1, QK_NOPE), _pad_lanes(q_norm_rope[l]), rope_tabs)
        ckv, kr, krr = _kv_norm(proj, kv_a_norm[l].reshape(1, KV_LORA), _pad_lanes(k_norm_rope[l]), rope_tabs)
        ckv_all = jnp.concatenate([ckv, cache_ckv[:, l].reshape(N_CACHE, KV_LORA)], axis=0)
        kv = _kv_up(ckv_all, _pack_w_kvb(w_kvb[l]), k_norm_nope[l].reshape(1, QK_NOPE))
        kr_cache = jnp.pad(cache_krope[:, l].reshape(N_CACHE, QK_ROPE),
                           ((0, 0), (0, V7X_LANES - QK_ROPE))).astype(BF16)
        kr_all = jnp.concatenate([krr, kr_cache], axis=0)
        g_mla = mla_out_norm[l].reshape(1, MLA_OUT)
        mix = _attention(q, kv, kr_all, g_mla, None, n_batch=BATCH, t_q=SEQ, q_row0=0,
                         segs=[(SEQ, 0)])
        mix = _attention(q, kv, kr_all, g_mla, mix, n_batch=DEC_BATCH, t_q=DEC_SEQ, q_row0=N_PROMPT,
                         segs=[(DEC_SEQ, N_PROMPT), (PAST_LEN, N_TOK)])

        wg, bg = _pack_gate(w_gf2[l], w_gb2[l], b_gf[l], b_gb[l])
        gn = gla_norm[l].reshape(1, GLA_DV)
        mix, s_f, s_b = _gla(proj, wg, bg, zero_state, zero_state, gn, mix, n_batch=BATCH, t=SEQ, row0=0)
        mix, _, _ = _gla(proj, wg, bg, state_gla_fwd[:, l], state_gla_bwd[:, l], gn, mix,
                         n_batch=DEC_BATCH, t=DEC_SEQ, row0=N_PROMPT)

        x = _matmul_resid(mix, w_o, l, x, mod, 2, name="out_proj")

        h2 = _normmod(x, norm2[l], mod, 3, 4)
        act = _matmul_ws(h2, w_up, l, tm=512, tn=1024, out_dtype=BF16, relu2=True, name="mlp_up")
        x = _matmul_resid(act, w_down, l, x, mod, 5, name="mlp_down")
        if l + 1 < DEPTH:
            h = _normmod(x, norm1[l + 1], mods[l + 1], 0, 1)

        ckvs.append(ckv[:N_PROMPT].reshape(BATCH, SEQ, KV_LORA))
        kropes.append(kr[:N_PROMPT, :QK_ROPE].reshape(BATCH, SEQ, QK_ROPE))
        sfs.append(s_f)
        sbs.append(s_b)

    return (x[:N_PROMPT].reshape(BATCH, SEQ, D_MODEL),
            x[N_PROMPT:].reshape(DEC_BATCH, DEC_SEQ, D_MODEL),
            jnp.stack(ckvs, axis=1), jnp.stack(kropes, axis=1),
            jnp.stack(sfs, axis=1), jnp.stack(sbs, axis=1))
```

```python
import functools

import jax
import jax.numpy as jnp
from jax import lax
from jax.experimental import pallas as pl
from jax.experimental.pallas import tpu as pltpu

D_MODEL = 4096
BATCH = 16
SEQ = 256
DEPTH = 2
DEC_BATCH = 4
DEC_SEQ = 1024
PAST_LEN = 512
GRID_W = 64
MLA_HEADS = 16
QK_NOPE = 128
QK_ROPE = 64
V_HEAD = 128
Q_LORA = 1024
KV_LORA = 512
GLA_HEADS = 8
GLA_DK = 128
GLA_DV = 256
GATE_RANK = 16
GATE_TAU = 16.0
CHUNK = 64
D_FF = 4 * D_MODEL
ROPE_BASE = 10000.0
EPS = 1e-6

MLA_OUT = MLA_HEADS * V_HEAD
GLA_QK = GLA_HEADS * GLA_DK
GLA_OUT = GLA_HEADS * GLA_DV
OFF_KV = Q_LORA
OFF_GQ = OFF_KV + KV_LORA + QK_ROPE
OFF_GK = OFF_GQ + GLA_QK
OFF_GV = OFF_GK + GLA_QK
OFF_GATE = OFF_GV + GLA_OUT
OFF_OG = OFF_GATE + 2 * GATE_RANK
N_IN = OFF_OG + GLA_OUT

F32 = jnp.float32
BF16 = jnp.bfloat16

V7X_LANES = 128
V7X_SCOPED_VMEM_BYTES = 60000 * 1024

N_PROMPT = BATCH * SEQ
N_SAMPLE = DEC_BATCH * DEC_SEQ
N_TOK = N_PROMPT + N_SAMPLE
N_CACHE = DEC_BATCH * PAST_LEN
N_MOD_ROWS = 8

P_KV = 1024
P_KROPE = P_KV + KV_LORA
P_GATE = P_KROPE + V7X_LANES
P_GQ = 2048
P_GK = P_GQ + GLA_QK
P_GV = P_GK + GLA_QK
P_OG = P_GV + GLA_OUT
P_TOTAL = P_OG + GLA_OUT
QK_PAD = 2 * V7X_LANES

ATTN_SCALE = float((QK_NOPE + QK_ROPE) ** -0.5)
GLA_SCALE = float(GLA_DK ** -0.5)


def _cparams(sem, vmem_bytes):
    return pltpu.CompilerParams(dimension_semantics=sem,
                                vmem_limit_bytes=min(int(vmem_bytes), V7X_SCOPED_VMEM_BYTES))


def _mod_row(row_block, rows_per_block):
    return jnp.maximum(row_block * rows_per_block // DEC_SEQ - (N_PROMPT // DEC_SEQ - 1), 0)


def _rms(x, g, n):
    ms = jnp.sum(x * x, axis=-1, keepdims=True) * (1.0 / n)
    return x * lax.rsqrt(ms + EPS) * g


def _ada_kernel(c_ref, w_ref, b_ref, o_ref):
    k = pl.program_id(2)

    @pl.when(k == 0)
    def _():
        o_ref[...] = jnp.broadcast_to(b_ref[...], o_ref.shape)

    c = c_ref[...]
    s = (c / (1.0 + jnp.exp(-c))).astype(BF16)
    o_ref[...] += jnp.dot(s, w_ref[...].astype(BF16), preferred_element_type=F32)


def _ada(cvec, w_ada, b_ada):
    tk, tn = 1024, 2048
    n_out = 6 * D_MODEL
    return pl.pallas_call(
        _ada_kernel,
        out_shape=jax.ShapeDtypeStruct((DEPTH, N_MOD_ROWS, n_out), F32),
        grid=(DEPTH, n_out // tn, D_MODEL // tk),
        in_specs=[
            pl.BlockSpec((N_MOD_ROWS, tk), lambda l, j, k: (0, k)),
            pl.BlockSpec((None, tk, tn), lambda l, j, k: (l, k, j)),
            pl.BlockSpec((None, 1, tn), lambda l, j, k: (l, 0, j)),
        ],
        out_specs=pl.BlockSpec((None, N_MOD_ROWS, tn), lambda l, j, k: (l, 0, j)),
        compiler_params=_cparams(("parallel", "parallel", "arbitrary"), 3 * tk * tn * 4),
        name="ada_mod",
    )(cvec, w_ada, b_ada.reshape(DEPTH, 1, n_out))


ROW_TILE = 512


def _stream_specs(x, tm=ROW_TILE, tn=D_MODEL, row_block0=0):
    col = (lambda idx: idx[1]) if tn != D_MODEL else (lambda idx: 0)
    if not isinstance(x, tuple):
        return [pl.BlockSpec((tm, tn), lambda *idx: (row_block0 + idx[0], col(idx)))], [x]
    assert row_block0 == 0
    pb = N_PROMPT // tm
    last_col = D_MODEL // tn - 1
    return ([pl.BlockSpec((tm, tn), lambda *idx: (jnp.minimum(idx[0], pb - 1),
                                                  jnp.where(idx[0] < pb, col(idx), last_col))),
             pl.BlockSpec((tm, tn), lambda *idx: (jnp.maximum(idx[0] - pb, 0),
                                                  jnp.where(idx[0] < pb, 0, col(idx))))], list(x))


def _load_stream(x_refs, tm=ROW_TILE):
    if len(x_refs) == 1:
        return x_refs[0][...]
    in_prompt = pl.program_id(0) < N_PROMPT // tm
    return jnp.where(in_prompt, x_refs[0][...], x_refs[1][...])


def _modulated_norm(x, g_ref, mod_ref, shift_idx, scale_idx):
    y = _rms(x, g_ref[...], D_MODEL)
    sc = mod_ref[0, scale_idx:scale_idx + 1, :]
    sh = mod_ref[0, shift_idx:shift_idx + 1, :]
    return (y * (1.0 + sc) + sh).astype(BF16)


def _normmod_kernel(*refs, n_x, shift_idx, scale_idx):
    g_ref, mod_ref, o_ref = refs[n_x:]
    o_ref[...] = _modulated_norm(_load_stream(refs[:n_x]), g_ref, mod_ref, shift_idx, scale_idx)


def _normmod(x, g, mod, shift_idx, scale_idx):
    x_specs, x_args = _stream_specs(x)
    row = pl.BlockSpec((ROW_TILE, D_MODEL), lambda i: (i, 0))
    return pl.pallas_call(
        functools.partial(_normmod_kernel, n_x=len(x_args), shift_idx=shift_idx, scale_idx=scale_idx),
        out_shape=jax.ShapeDtypeStruct((N_TOK, D_MODEL), BF16),
        grid=(N_TOK // ROW_TILE,),
        in_specs=x_specs + [
            pl.BlockSpec((1, D_MODEL), lambda i: (0, 0)),
            pl.BlockSpec((1, 6, D_MODEL), lambda i: (_mod_row(i, ROW_TILE), 0, 0)),
        ],
        out_specs=row,
        compiler_params=_cparams(("arbitrary",), 8 * ROW_TILE * D_MODEL * 4),
        name="norm_mod",
    )(*x_args, g.reshape(1, D_MODEL), mod)


def _mm_resid_kernel(a_ref, b_ref, *rest, nk, n_x, tm, gate_idx):
    x_refs, mod_refs, o_ref = rest[:n_x], rest[n_x:-1], rest[-1]
    k = pl.program_id(2)

    def prod():
        return jnp.dot(a_ref[...], b_ref[...].astype(BF16), preferred_element_type=F32)

    @pl.when(k == 0)
    def _():
        o_ref[...] = prod()

    @pl.when(jnp.logical_and(k > 0, k < nk - 1))
    def _():
        o_ref[...] += prod()

    @pl.when(k == nk - 1)
    def _():
        upd = o_ref[...] + prod()
        x = _load_stream(x_refs, tm)
        rows = tm // len(mod_refs)
        for part, mod_ref in enumerate(mod_refs):
            sl = slice(part * rows, (part + 1) * rows)
            o_ref[sl, :] = x[sl, :] + mod_ref[0, gate_idx:gate_idx + 1, :] * upd[sl, :]


def _matmul_resid(a, b, layer, x, mod, gate_idx, *, row0=0, rows=N_TOK, name):
    tm, tn, tk = (1024 if isinstance(x, tuple) else 2048), 1024, 1024
    kdim = a.shape[1]
    m = rows
    n = b.shape[-1]
    nk = kdim // tk
    rb0 = row0 // tm
    x_specs, x_args = _stream_specs(x, tm, tn, rb0)
    n_mod = tm // DEC_SEQ
    mod_specs = [pl.BlockSpec((1, 6, tn),
                              lambda i, j, k, p=p: (_mod_row((rb0 + i) * n_mod + p, DEC_SEQ), 0, j))
                 for p in range(n_mod)]
    vmem = (2 * (tm * tk * 2 + tk * tn * b.dtype.itemsize + (1 + len(x_args)) * tm * tn * 4)
            + 2 * tk * tn * 2 + 2 * tm * tk * 2)
    return pl.pallas_call(
        functools.partial(_mm_resid_kernel, nk=nk, n_x=len(x_args), tm=tm, gate_idx=gate_idx),
        out_shape=jax.ShapeDtypeStruct((m, n), F32),
        grid=(m // tm, n // tn, nk),
        in_specs=[pl.BlockSpec((tm, tk), lambda i, j, k: (rb0 + i, k)),
                  pl.BlockSpec((None, tk, tn), lambda i, j, k: (layer, k, j))] + x_specs + mod_specs,
        out_specs=pl.BlockSpec((tm, tn), lambda i, j, k: (i, j)),
        compiler_params=_cparams(("arbitrary", "arbitrary", "arbitrary"), vmem),
        name=name,
    )(a, b, *x_args, *([mod] * n_mod))


def _mm_ws_kernel(a_ref, b_ref, o_ref, *scratch, relu2, b_transposed):
    if scratch:
        bw_ref = scratch[0]

        @pl.when(pl.program_id(1) == 0)
        def _():
            bw_ref[...] = b_ref[...].astype(BF16)

        b = bw_ref[...]
    else:
        b = b_ref[...]
    contract_b = 1 if b_transposed else 0
    r = lax.dot_general(a_ref[...], b, (((1,), (contract_b,)), ((), ())), preferred_element_type=F32)
    if relu2:
        r = jnp.square(jnp.maximum(r, 0.0))
    o_ref[...] = r.astype(o_ref.dtype)


def _matmul_ws(a, b, layer, *, tm, tn, out_dtype, relu2=False, b_transposed=False, name):
    m, kdim = a.shape
    if b_transposed:
        n = b.shape[0]
        b_spec = pl.BlockSpec((tn, kdim), lambda j, i: (j, 0))
    elif layer is None:
        n = b.shape[-1]
        b_spec = pl.BlockSpec((kdim, tn), lambda j, i: (0, j))
    else:
        n = b.shape[-1]
        b_spec = pl.BlockSpec((None, kdim, tn), lambda j, i: (layer, 0, j))
    cast = b.dtype != jnp.dtype(BF16)
    o_bytes = jnp.dtype(out_dtype).itemsize
    vmem = (2 * (tm * kdim * 2 + kdim * tn * b.dtype.itemsize + tm * tn * o_bytes)
            + cast * kdim * tn * 2 + 2 * tm * tn * 4)
    return pl.pallas_call(
        functools.partial(_mm_ws_kernel, relu2=relu2, b_transposed=b_transposed),
        out_shape=jax.ShapeDtypeStruct((m, n), out_dtype),
        grid=(n // tn, m // tm),
        in_specs=[pl.BlockSpec((tm, kdim), lambda j, i: (i, 0)), b_spec],
        out_specs=pl.BlockSpec((tm, tn), lambda j, i: (i, j)),
        scratch_shapes=[pltpu.VMEM((tn, kdim) if b_transposed else (kdim, tn), BF16)] if cast else [],
        compiler_params=_cparams(("parallel", "arbitrary"), vmem),
        name=name,
    )(a, b)


def _rope128(r, cos, sa, sb):
    return r * cos + pltpu.roll(r, 96, 1) * sa + pltpu.roll(r, 32, 1) * sb


def _q_kernel(ql_ref, ga_ref, w_ref, gn_ref, gr_ref, cos_ref, sa_ref, sb_ref, o_ref, qn_ref, *, heads_per_tile):
    @pl.when(pl.program_id(1) == 0)
    def _():
        qn_ref[...] = _rms(ql_ref[...], ga_ref[...], Q_LORA).astype(BF16)

    cos, sa, sb = cos_ref[...], sa_ref[...], sb_ref[...]
    qn = qn_ref[...]
    for hh in range(heads_per_tile):
        c0 = hh * QK_PAD
        acc = jnp.dot(qn, w_ref[:, c0:c0 + QK_PAD], preferred_element_type=F32)
        nope = acc[:, :QK_NOPE]
        rope = acc[:, QK_NOPE:]
        nope = _rms(nope, gn_ref[...], QK_NOPE) * ATTN_SCALE
        rope = _rope128(_rms(rope, gr_ref[...], QK_ROPE), cos, sa, sb) * ATTN_SCALE
        o_ref[:, c0:c0 + QK_NOPE] = nope.astype(BF16)
        o_ref[:, c0 + QK_NOPE:c0 + QK_PAD] = rope.astype(BF16)


def _rope_specs(tm, nargs):
    blocks_per_seq = DEC_SEQ // tm

    def idx(i, *_):
        return (jnp.minimum(i * tm // N_PROMPT, 1), i % blocks_per_seq, 0)

    return [pl.BlockSpec((None, tm, V7X_LANES), idx) for _ in range(nargs)]


def _q_proj(proj, ga, w_qb_p, gn, gr128, rope_tabs):
    tm, tn = 512, 1024
    n = MLA_HEADS * QK_PAD
    return pl.pallas_call(
        functools.partial(_q_kernel, heads_per_tile=tn // QK_PAD),
        out_shape=jax.ShapeDtypeStruct((N_TOK, n), BF16),
        grid=(N_TOK // tm, n // tn),
        in_specs=[
            pl.BlockSpec((tm, Q_LORA), lambda i, j: (i, 0)),
            pl.BlockSpec((1, Q_LORA), lambda i, j: (0, 0)),
            pl.BlockSpec((Q_LORA, tn), lambda i, j: (0, j)),
            pl.BlockSpec((1, QK_NOPE), lambda i, j: (0, 0)),
            pl.BlockSpec((1, V7X_LANES), lambda i, j: (0, 0)),
        ] + _rope_specs(tm, 3),
        out_specs=pl.BlockSpec((tm, tn), lambda i, j: (i, j)),
        scratch_shapes=[pltpu.VMEM((tm, Q_LORA), BF16)],
        compiler_params=_cparams(
            ("parallel", "arbitrary"),
            2 * (tm * Q_LORA * 4 + Q_LORA * tn * 2 + tm * tn * 2) + tm * Q_LORA * 2 + 3 * tm * tn * 4),
        name="mla_q_proj",
    )(proj, ga, w_qb_p, gn, gr128, *rope_tabs)


def _kvnorm_kernel(c_ref, r_ref, gc_ref, gk_ref, cos_ref, sa_ref, sb_ref, ckv_ref, kr_ref, krr_ref):
    ckv_ref[...] = _rms(c_ref[...], gc_ref[...], KV_LORA)
    rn = _rms(r_ref[...], gk_ref[...], QK_ROPE)
    kr_ref[...] = rn
    krr_ref[...] = _rope128(rn, cos_ref[...], sa_ref[...], sb_ref[...]).astype(BF16)


def _kv_norm(proj, gc, gk128, rope_tabs):
    tm = 512
    return pl.pallas_call(
        _kvnorm_kernel,
        out_shape=(jax.ShapeDtypeStruct((N_TOK, KV_LORA), F32),
                   jax.ShapeDtypeStruct((N_TOK, V7X_LANES), F32),
                   jax.ShapeDtypeStruct((N_TOK, V7X_LANES), BF16)),
        grid=(N_TOK // tm,),
        in_specs=[
            pl.BlockSpec((tm, KV_LORA), lambda i: (i, P_KV // KV_LORA)),
            pl.BlockSpec((tm, V7X_LANES), lambda i: (i, P_KROPE // V7X_LANES)),
            pl.BlockSpec((1, KV_LORA), lambda i: (0, 0)),
            pl.BlockSpec((1, V7X_LANES), lambda i: (0, 0)),
        ] + _rope_specs(tm, 3),
        out_specs=(pl.BlockSpec((tm, KV_LORA), lambda i: (i, 0)),
                   pl.BlockSpec((tm, V7X_LANES), lambda i: (i, 0)),
                   pl.BlockSpec((tm, V7X_LANES), lambda i: (i, 0))),
        compiler_params=_cparams(("parallel",), 8 * tm * (KV_LORA + 4 * V7X_LANES) * 4),
        name="mla_kv_norm",
    )(proj, proj, gc, gk128, *rope_tabs)


def _kvup_kernel(own_ref, cache_ref, w_ref, gn_ref, o_ref, *, heads_per_tile, k_tiles, own_blocks):
    c = jnp.where(pl.program_id(0) < own_blocks, own_ref[...], cache_ref[...]).astype(BF16)
    pair = 2 * QK_NOPE

    @pl.when(pl.program_id(1) < k_tiles)
    def _():
        for pp in range(heads_per_tile // 2):
            acc = jnp.dot(c, w_ref[:, pp * pair:(pp + 1) * pair], preferred_element_type=F32)
            for hh in range(2):
                c0 = pp * pair + hh * QK_NOPE
                o_ref[:, c0:c0 + QK_NOPE] = _rms(
                    acc[:, hh * QK_NOPE:(hh + 1) * QK_NOPE], gn_ref[...], QK_NOPE).astype(BF16)

    @pl.when(pl.program_id(1) >= k_tiles)
    def _():
        o_ref[...] = jnp.dot(c, w_ref[...], preferred_element_type=F32).astype(BF16)


def _kv_up(ckv, cache, w_kvb_p, gn):
    tm, tn = 512, 1024
    own_blocks = N_TOK // tm
    m = N_TOK + N_CACHE
    n = 2 * MLA_OUT
    return pl.pallas_call(
        functools.partial(_kvup_kernel, heads_per_tile=tn // QK_NOPE, k_tiles=MLA_OUT // tn,
                          own_blocks=own_blocks),
        out_shape=jax.ShapeDtypeStruct((m, n), BF16),
        grid=(m // tm, n // tn),
        in_specs=[
            pl.BlockSpec((tm, KV_LORA), lambda i, j: (jnp.minimum(i, own_blocks - 1), 0)),
            pl.BlockSpec((tm, KV_LORA), lambda i, j: (jnp.maximum(i - own_blocks, 0), 0)),
            pl.BlockSpec((KV_LORA, tn), lambda i, j: (0, j)),
            pl.BlockSpec((1, QK_NOPE), lambda i, j: (0, 0)),
        ],
        out_specs=pl.BlockSpec((tm, tn), lambda i, j: (i, j)),
        compiler_params=_cparams(
            ("arbitrary", "arbitrary"),
            2 * (2 * tm * KV_LORA * 4 + KV_LORA * tn * 2 + tm * tn * 2) + 3 * tm * tn * 4),
        name="mla_kv_up",
    )(ckv, cache, w_kvb_p, gn)


def _attn_kernel(*refs, n_seg):
    q_ref = refs[0]
    seg_refs = [refs[1 + 3 * s:4 + 3 * s] for s in range(n_seg)]
    g_ref = refs[1 + 3 * n_seg]
    o_ref, oacc_ref = refs[-2:]
    def head_scores(h):
        qh = q_ref[:, h * QK_PAD:(h + 1) * QK_PAD]
        out = []
        for kn_ref, kr_ref, _ in seg_refs:
            kh = jnp.concatenate([kn_ref[:, h * QK_NOPE:(h + 1) * QK_NOPE], kr_ref[...]], axis=1)
            out.append(lax.dot_general(qh, kh, (((1,), (1,)), ((), ())), preferred_element_type=F32))
        return out

    next_scores = head_scores(0)
    for h in range(MLA_HEADS):
        scores = next_scores
        if h + 1 < MLA_HEADS:
            next_scores = head_scores(h + 1)
        m = scores[0].max(axis=-1, keepdims=True)
        for s in scores[1:]:
            m = jnp.maximum(m, s.max(axis=-1, keepdims=True))
        denom = None
        out = None
        for s, (_, _, v_ref) in zip(scores, seg_refs):
            p = jnp.exp(s - m)
            ps = p.sum(axis=-1, keepdims=True)
            pv = jnp.dot(p.astype(BF16), v_ref[:, h * V_HEAD:(h + 1) * V_HEAD], preferred_element_type=F32)
            denom = ps if denom is None else denom + ps
            out = pv if out is None else out + pv
        oacc_ref[:, h * V_HEAD:(h + 1) * V_HEAD] = out / denom
    o_ref[...] = _rms(oacc_ref[...], g_ref[...], MLA_OUT).astype(BF16)


def _mix_alias(mix, args, in_specs):
    if mix is None:
        return {}
    args.append(mix)
    in_specs.append(pl.BlockSpec(memory_space=pl.ANY))
    return {len(args) - 1: 0}


def _attention(q, kv, kr, g_out, mix, *, n_batch, t_q, segs, q_row0):
    tq = 256
    qb = t_q // tq
    in_specs = [pl.BlockSpec((tq, MLA_HEADS * QK_PAD), lambda b, i: (q_row0 // tq + b * qb + i, 0))]
    args = [q]
    vmem = 2 * tq * MLA_HEADS * QK_PAD * 2
    for rows, row0 in segs:
        blk0 = row0 // rows
        in_specs += [
            pl.BlockSpec((rows, MLA_OUT), lambda b, i, blk0=blk0: (blk0 + b, 0)),
            pl.BlockSpec((rows, V7X_LANES), lambda b, i, blk0=blk0: (blk0 + b, 0)),
            pl.BlockSpec((rows, MLA_OUT), lambda b, i, blk0=blk0: (blk0 + b, 1)),
        ]
        args += [kv, kr, kv]
        vmem += 2 * rows * (2 * MLA_OUT + V7X_LANES) * 2 + 12 * tq * rows * 4
    in_specs.append(pl.BlockSpec((1, MLA_OUT), lambda b, i: (0, 0)))
    args.append(g_out)
    vmem += 4 * tq * MLA_OUT * 4
    aliases = _mix_alias(mix, args, in_specs)
    return pl.pallas_call(
        functools.partial(_attn_kernel, n_seg=len(segs)),
        out_shape=jax.ShapeDtypeStruct((N_TOK, MLA_OUT + GLA_OUT), BF16),
        grid=(n_batch, qb),
        in_specs=in_specs,
        out_specs=pl.BlockSpec((tq, MLA_OUT), lambda b, i: (q_row0 // tq + b * qb + i, 0)),
        scratch_shapes=[pltpu.VMEM((tq, MLA_OUT), F32)],
        input_output_aliases=aliases,
        compiler_params=_cparams(("parallel", "arbitrary"), vmem),
        name="mla_attention_%d" % t_q,
    )(*args)


def _log_sigmoid(x):
    return -(jnp.maximum(-x, 0.0) + jnp.log(1.0 + jnp.exp(-jnp.abs(x))))


def _exact_tri_dot(keep, g):
    tri = jnp.where(keep, 1.0, 0.0).astype(BF16)
    g_hi = g.astype(BF16)
    rem = g - g_hi.astype(F32)
    g_mid = rem.astype(BF16)
    g_lo = (rem - g_mid.astype(F32)).astype(BF16)
    return (jnp.dot(tri, g_hi, preferred_element_type=F32)
            + jnp.dot(tri, g_mid, preferred_element_type=F32)
            + jnp.dot(tri, g_lo, preferred_element_type=F32))


def _gla_head_group(t):
    return GLA_HEADS if t <= SEQ else GLA_HEADS // 2


def _gla_kernel(q_ref, k_ref, v_ref, glr_ref, wgf_ref, wgb_ref, bgf_ref, bgb_ref, s0f_ref, s0b_ref,
                og_ref, gn_ref, *rest, n_chunks, hg):
    o_ref, sf_ref, sb_ref, gf_s, gb_s, of_s, ob_s, st_s = rest[-8:]
    glr = glr_ref[...].astype(BF16)

    def gate(w_ref, b_ref):
        x = jnp.dot(glr, w_ref[...].astype(BF16), preferred_element_type=F32) + b_ref[...]
        return _log_sigmoid(x) * (1.0 / GATE_TAU)

    gf_s[...] = gate(wgf_ref, bgf_ref)
    gb_s[...] = gate(wgb_ref, bgb_ref)
    for j in range(hg):
        st_s[j] = s0f_ref[j].T
        st_s[hg + j] = s0b_ref[j].T

    row = lax.broadcasted_iota(jnp.int32, (CHUNK, CHUNK), 0)
    col = lax.broadcasted_iota(jnp.int32, (CHUNK, CHUNK), 1)
    nt = (((1,), (1,)), ((), ()))
    tn = (((0,), (0,)), ((), ()))

    keep_f = col <= row
    keep_b = col >= row

    def cumulate(n, carry):
        sl = pl.ds(pl.multiple_of(n * CHUNK, CHUNK), CHUNK)
        gf_s[sl, :] = _exact_tri_dot(keep_f, gf_s[sl, :])
        gb_s[sl, :] = _exact_tri_dot(keep_b, gb_s[sl, :])
        return carry

    lax.fori_loop(0, n_chunks, cumulate, 0)

    def step(n, carry):
        units = []
        for j in range(hg):
            units.append((n, j, j, gf_s, of_s, keep_f, CHUNK - 1, CHUNK // 2 - 1))
            units.append((n_chunks - 1 - n, j, hg + j, gb_s, ob_s, keep_b, 0, CHUNK // 2))
        staged = []
        for cn, j, slot, b_s, o_s, keep, end_idx, mid_idx in units:
            sl = pl.ds(pl.multiple_of(cn * CHUNK, CHUNK), CHUNK)
            dk = slice(j * GLA_DK, (j + 1) * GLA_DK)
            dv = slice(j * GLA_DV, (j + 1) * GLA_DV)
            b = b_s[sl, dk]
            q = q_ref[sl, dk] * GLA_SCALE
            k = k_ref[sl, dk]
            v = v_ref[sl, dv].astype(BF16)
            b_end = b[end_idx:end_idx + 1, :]
            b_mid = b[mid_idx:mid_idx + 1, :]
            qe = (q * jnp.exp(b)).astype(BF16)
            qa = (q * jnp.exp(b - b_mid)).astype(BF16)
            ka = (k * jnp.exp(b_mid - b)).astype(BF16)
            kd = (k * jnp.exp(b_end - b)).astype(BF16)
            staged.append((sl, dv, slot, o_s, keep, v, qe, qa, ka, kd, jnp.exp(b_end)))
        scores = [lax.dot_general(qa, ka, nt, preferred_element_type=F32)
                  for _, _, _, _, _, _, _, qa, ka, _, _ in staged]
        inter = [lax.dot_general(qe, st_s[slot].astype(BF16), nt, preferred_element_type=F32)
                 for _, _, slot, _, _, _, qe, _, _, _, _ in staged]
        update = [lax.dot_general(v, kd, tn, preferred_element_type=F32)
                  for _, _, _, _, _, v, _, _, _, kd, _ in staged]
        for (sl, dv, slot, o_s, keep, v, _, _, _, _, decay), a, o_inter, upd in zip(staged, scores, inter, update):
            a = jnp.where(keep, a, 0.0).astype(BF16)
            o_s[sl, dv] = jnp.dot(a, v, preferred_element_type=F32) + o_inter
            st_s[slot] = st_s[slot] * decay + upd
        return carry

    lax.fori_loop(0, n_chunks, step, 0)
    for j in range(hg):
        sf_ref[j] = st_s[j].T
        sb_ref[j] = st_s[hg + j].T
        dv = slice(j * GLA_DV, (j + 1) * GLA_DV)
        og = og_ref[:, dv]
        o = _rms(of_s[:, dv] + ob_s[:, dv], gn_ref[...], GLA_DV)
        o_ref[:, dv] = (o * (og / (1.0 + jnp.exp(-og)))).astype(BF16)


def _gla(proj, wg, bg, s0f, s0b, gn, mix, *, n_batch, t, row0):
    hg = _gla_head_group(t)
    rb0 = row0 // t
    n_groups = GLA_HEADS // hg
    tok = lambda width, col0: pl.BlockSpec(
        (t, hg * width), lambda b, h, cb=col0 // (hg * width): (rb0 + b, cb + h))
    state = pl.BlockSpec((None, hg, GLA_DK, GLA_DV), lambda b, h: (b, h, 0, 0))
    in_specs = [
        tok(GLA_DK, P_GQ),
        tok(GLA_DK, P_GK),
        tok(GLA_DV, P_GV),
        pl.BlockSpec((t, V7X_LANES), lambda b, h: (rb0 + b, P_GATE // V7X_LANES)),
        pl.BlockSpec((V7X_LANES, hg * GLA_DK), lambda b, h: (0, h)),
        pl.BlockSpec((V7X_LANES, hg * GLA_DK), lambda b, h: (0, n_groups + h)),
        pl.BlockSpec((1, hg * GLA_DK), lambda b, h: (0, h)),
        pl.BlockSpec((1, hg * GLA_DK), lambda b, h: (0, n_groups + h)),
        state, state,
        tok(GLA_DV, P_OG),
        pl.BlockSpec((1, GLA_DV), lambda b, h: (0, 0)),
    ]
    args = [proj, proj, proj, proj, wg, wg, bg, bg, s0f, s0b, proj, gn]
    aliases = _mix_alias(mix, args, in_specs)
    return pl.pallas_call(
        functools.partial(_gla_kernel, n_chunks=t // CHUNK, hg=hg),
        out_shape=(jax.ShapeDtypeStruct((N_TOK, MLA_OUT + GLA_OUT), BF16),
                   jax.ShapeDtypeStruct((n_batch, GLA_HEADS, GLA_DK, GLA_DV), F32),
                   jax.ShapeDtypeStruct((n_batch, GLA_HEADS, GLA_DK, GLA_DV), F32)),
        grid=(n_batch, n_groups),
        in_specs=in_specs,
        input_output_aliases=aliases,
        out_specs=(pl.BlockSpec((t, hg * GLA_DV),
                                lambda b, h: (rb0 + b, MLA_OUT // (hg * GLA_DV) + h)), state, state),
        scratch_shapes=[pltpu.VMEM((t, hg * GLA_DK), F32), pltpu.VMEM((t, hg * GLA_DK), F32),
                        pltpu.VMEM((t, hg * GLA_DV), F32), pltpu.VMEM((t, hg * GLA_DV), F32),
                        pltpu.VMEM((2 * hg, GLA_DV, GLA_DK), F32)],
        compiler_params=_cparams(
            ("parallel", "parallel"),
            2 * (t * hg * (2 * GLA_DK + 2 * GLA_DV) * 4 + t * V7X_LANES * 4 + t * hg * GLA_DV * 2)
            + t * hg * (2 * GLA_DK + 2 * GLA_DV) * 4 + 6 * t * hg * GLA_DK * 4
            + 10 * hg * GLA_DK * GLA_DV * 4 + 4 * V7X_LANES * hg * GLA_DK * 2),
        name="gla_%d" % t,
    )(*args)


def _pack_w_in_kernel(w_ref, o_ref):
    segments = ((0, 0, OFF_KV + KV_LORA), (P_KROPE, OFF_KV + KV_LORA, QK_ROPE), (P_GATE, OFF_GATE, 2 * GATE_RANK),
                (P_GQ, OFF_GQ, OFF_GATE - OFF_GQ), (P_OG, OFF_OG, GLA_OUT))
    cols = o_ref.shape[1]
    o_ref[P_KROPE:P_GQ, :] = jnp.zeros((P_GQ - P_KROPE, cols), BF16)
    for dst, src, rows in segments:
        o_ref[dst:dst + rows, :] = w_ref[src:src + rows, :].astype(BF16)


def _pack_w_in(w_in_t, layer):
    tc = 256
    return pl.pallas_call(
        _pack_w_in_kernel,
        out_shape=jax.ShapeDtypeStruct((P_TOTAL, D_MODEL), BF16),
        grid=(D_MODEL // tc,),
        in_specs=[pl.BlockSpec((None, N_IN, tc), lambda i: (layer, 0, i))],
        out_specs=pl.BlockSpec((P_TOTAL, tc), lambda i: (0, i)),
        compiler_params=_cparams(("parallel",), 2 * tc * (N_IN * 4 + P_TOTAL * 2) + 2 * tc * P_TOTAL * 4),
        name="pack_w_in",
    )(w_in_t)


def _pack_w_qb(w):
    w = w.reshape(Q_LORA, MLA_HEADS, QK_NOPE + QK_ROPE)
    w = jnp.pad(w, ((0, 0), (0, 0), (0, QK_PAD - QK_NOPE - QK_ROPE)))
    return w.reshape(Q_LORA, MLA_HEADS * QK_PAD).astype(BF16)


def _pack_w_kvb(w):
    w = w.reshape(KV_LORA, MLA_HEADS, QK_NOPE + V_HEAD)
    return jnp.concatenate([w[:, :, :QK_NOPE].reshape(KV_LORA, MLA_OUT),
                            w[:, :, QK_NOPE:].reshape(KV_LORA, MLA_OUT)], axis=1).astype(BF16)


def _pack_gate(w_gf2, w_gb2, b_gf, b_gb):
    wg = jnp.zeros((V7X_LANES, 2 * GLA_QK), F32)
    wg = wg.at[:GATE_RANK, :GLA_QK].set(w_gf2).at[GATE_RANK:2 * GATE_RANK, GLA_QK:].set(w_gb2)
    return wg.astype(BF16), jnp.concatenate([b_gf, b_gb]).reshape(1, 2 * GLA_QK)


def _pad_lanes(g):
    return jnp.pad(g, (0, V7X_LANES - g.shape[0])).reshape(1, V7X_LANES)


def _rope_tables():
    rows = DEC_SEQ // GRID_W
    row = jnp.repeat(jnp.arange(rows, dtype=F32), GRID_W)
    col = jnp.tile(jnp.arange(GRID_W, dtype=F32), rows)
    n_freq = QK_ROPE // 4
    inv = jnp.power(ROPE_BASE, -jnp.arange(n_freq, dtype=F32) / n_freq)
    ang = jnp.concatenate([row[:, None] * inv, col[:, None] * inv], axis=-1)
    cos, sin = jnp.cos(ang), jnp.sin(ang)
    half = QK_ROPE // 2
    zpad = jnp.zeros((DEC_SEQ, V7X_LANES - QK_ROPE), F32)
    zhalf = jnp.zeros((DEC_SEQ, half), F32)
    cos_t = jnp.concatenate([cos, cos, zpad], axis=1)
    sa_t = jnp.concatenate([-sin, zhalf, zpad], axis=1)
    sb_t = jnp.concatenate([zhalf, sin, zpad], axis=1)
    ident = jnp.concatenate([jnp.ones((DEC_SEQ, QK_ROPE), F32), zpad], axis=1)
    zeros = jnp.zeros((DEC_SEQ, V7X_LANES), F32)
    return (jnp.stack([ident, cos_t]), jnp.stack([zeros, sa_t]), jnp.stack([zeros, sb_t]))


def kernel(x_prompt, x_sample, c, cache_ckv, cache_krope, state_gla_fwd, state_gla_bwd, c_ctx, w_ada, b_ada, norm1, norm2, w_in, q_a_norm, w_qb, kv_a_norm, w_kvb, q_norm_nope, q_norm_rope, k_norm_nope, k_norm_rope, w_gf2, b_gf, w_gb2, b_gb, gla_norm, mla_out_norm, w_o, w_up, w_down):
    x = (x_prompt.reshape(N_PROMPT, D_MODEL), x_sample.reshape(N_SAMPLE, D_MODEL))
    cvec = jnp.concatenate([c_ctx[None, :], c, jnp.zeros((N_MOD_ROWS - 1 - DEC_BATCH, D_MODEL), F32)], axis=0)
    mods = _ada(cvec, w_ada, b_ada).reshape(DEPTH, N_MOD_ROWS, 6, D_MODEL)
    rope_tabs = _rope_tables()
    zero_state = jnp.zeros((BATCH, GLA_HEADS, GLA_DK, GLA_DV), F32)
    w_in_t = jnp.swapaxes(w_in, 1, 2)

    ckvs, kropes, sfs, sbs = [], [], [], []
    h = _normmod(x, norm1[0], mods[0], 0, 1)
    for l in range(DEPTH):
        mod = mods[l]
        proj = _matmul_ws(h, _pack_w_in(w_in_t, l), None, tm=1024, tn=1024, out_dtype=F32,
                          b_transposed=True, name="in_proj")

        q = _q_proj(proj, q_a_norm[l].reshape(1, Q_LORA), _pack_w_qb(w_qb[l]),
                    q_norm_nope[l].reshape(1, QK_NOPE), _pad_lanes(q_norm_rope[l]), rope_tabs)
        ckv, kr, krr = _kv_norm(proj, kv_a_norm[l].reshape(1, KV_LORA), _pad_lanes(k_norm_rope[l]), rope_tabs)
        kv = _kv_up(ckv, cache_ckv[:, l].reshape(N_CACHE, KV_LORA), _pack_w_kvb(w_kvb[l]),
                    k_norm_nope[l].reshape(1, QK_NOPE))
        kr_cache = jnp.pad(cache_krope[:, l].reshape(N_CACHE, QK_ROPE),
                           ((0, 0), (0, V7X_LANES - QK_ROPE))).astype(BF16)
        kr_all = jnp.concatenate([krr, kr_cache], axis=0)
        g_mla = mla_out_norm[l].reshape(1, MLA_OUT)
        mix = _attention(q, kv, kr_all, g_mla, None, n_batch=BATCH, t_q=SEQ, q_row0=0,
                         segs=[(SEQ, 0)])
        mix = _attention(q, kv, kr_all, g_mla, mix, n_batch=DEC_BATCH, t_q=DEC_SEQ, q_row0=N_PROMPT,
                         segs=[(DEC_SEQ, N_PROMPT), (PAST_LEN, N_TOK)])

        wg, bg = _pack_gate(w_gf2[l], w_gb2[l], b_gf[l], b_gb[l])
        gn = gla_norm[l].reshape(1, GLA_DV)
        mix, s_f, s_b = _gla(proj, wg, bg, zero_state, zero_state, gn, mix, n_batch=BATCH, t=SEQ, row0=0)
        mix, _, _ = _gla(proj, wg, bg, state_gla_fwd[:, l], state_gla_bwd[:, l], gn, mix,
                         n_batch=DEC_BATCH, t=DEC_SEQ, row0=N_PROMPT)

        x = _matmul_resid(mix, w_o, l, x, mod, 2, name="out_proj")

        h2 = _normmod(x, norm2[l], mod, 3, 4)
        act = _matmul_ws(h2, w_up, l, tm=512, tn=1024, out_dtype=BF16, relu2=True, name="mlp_up")
        if l + 1 < DEPTH:
            x = _matmul_resid(act, w_down, l, x, mod, 5, name="mlp_down")
            h = _normmod(x, norm1[l + 1], mods[l + 1], 0, 1)
        else:
            y_prompt = _matmul_resid(act, w_down, l, x, mod, 5, row0=0, rows=N_PROMPT, name="mlp_down_ctx")
            y_sample = _matmul_resid(act, w_down, l, x, mod, 5, row0=N_PROMPT, rows=N_SAMPLE, name="mlp_down_lat")

        ckvs.append(ckv[:N_PROMPT].reshape(BATCH, SEQ, KV_LORA))
        kropes.append(kr[:N_PROMPT, :QK_ROPE].reshape(BATCH, SEQ, QK_ROPE))
        sfs.append(s_f)
        sbs.append(s_b)

    return (y_prompt.reshape(BATCH, SEQ, D_MODEL),
            y_sample.reshape(DEC_BATCH, DEC_SEQ, D_MODEL),
            jnp.stack(ckvs, axis=1), jnp.stack(kropes, axis=1),
            jnp.stack(sfs, axis=1), jnp.stack(sbs, axis=1))
```

```python
import functools

import jax
import jax.numpy as jnp
from jax import lax
from jax.experimental import pallas as pl
from jax.experimental.pallas import tpu as pltpu

D_MODEL = 4096
BATCH = 16
SEQ = 256
DEPTH = 2
DEC_BATCH = 4
DEC_SEQ = 1024
PAST_LEN = 512
GRID_W = 64
MLA_HEADS = 16
QK_NOPE = 128
QK_ROPE = 64
V_HEAD = 128
Q_LORA = 1024
KV_LORA = 512
GLA_HEADS = 8
GLA_DK = 128
GLA_DV = 256
GATE_RANK = 16
GATE_TAU = 16.0
CHUNK = 64
D_FF = 4 * D_MODEL
ROPE_BASE = 10000.0
EPS = 1e-6

MLA_OUT = MLA_HEADS * V_HEAD
GLA_QK = GLA_HEADS * GLA_DK
GLA_OUT = GLA_HEADS * GLA_DV
OFF_KV = Q_LORA
OFF_GQ = OFF_KV + KV_LORA + QK_ROPE
OFF_GK = OFF_GQ + GLA_QK
OFF_GV = OFF_GK + GLA_QK
OFF_GATE = OFF_GV + GLA_OUT
OFF_OG = OFF_GATE + 2 * GATE_RANK
N_IN = OFF_OG + GLA_OUT

F32 = jnp.float32
BF16 = jnp.bfloat16

V7X_LANES = 128
V7X_SCOPED_VMEM_BYTES = 60000 * 1024

N_PROMPT = BATCH * SEQ
N_SAMPLE = DEC_BATCH * DEC_SEQ
N_TOK = N_PROMPT + N_SAMPLE
N_CACHE = DEC_BATCH * PAST_LEN
N_MOD_ROWS = 8

P_KV = 1024
P_KROPE = P_KV + KV_LORA
P_GATE = P_KROPE + V7X_LANES
P_GQ = 2048
P_GK = P_GQ + GLA_QK
P_GV = P_GK + GLA_QK
P_OG = P_GV + GLA_OUT
P_TOTAL = P_OG + GLA_OUT
QK_PAD = 2 * V7X_LANES

ATTN_SCALE = float((QK_NOPE + QK_ROPE) ** -0.5)
GLA_SCALE = float(GLA_DK ** -0.5)


def _cparams(sem, vmem_bytes):
    return pltpu.CompilerParams(dimension_semantics=sem,
                                vmem_limit_bytes=min(int(vmem_bytes), V7X_SCOPED_VMEM_BYTES))


def _mod_row(row_block, rows_per_block):
    return jnp.maximum(row_block * rows_per_block // DEC_SEQ - (N_PROMPT // DEC_SEQ - 1), 0)


def _rms(x, g, n):
    ms = jnp.sum(x * x, axis=-1, keepdims=True) * (1.0 / n)
    return x * lax.rsqrt(ms + EPS) * g


def _ada_kernel(c_ref, w_ref, b_ref, o_ref):
    k = pl.program_id(2)

    @pl.when(k == 0)
    def _():
        o_ref[...] = jnp.broadcast_to(b_ref[...], o_ref.shape)

    c = c_ref[...]
    s = (c / (1.0 + jnp.exp(-c))).astype(BF16)
    o_ref[...] += jnp.dot(s, w_ref[...].astype(BF16), preferred_element_type=F32)


def _ada(cvec, w_ada, b_ada):
    tk, tn = 1024, 2048
    n_out = 6 * D_MODEL
    return pl.pallas_call(
        _ada_kernel,
        out_shape=jax.ShapeDtypeStruct((DEPTH, N_MOD_ROWS, n_out), F32),
        grid=(DEPTH, n_out // tn, D_MODEL // tk),
        in_specs=[
            pl.BlockSpec((N_MOD_ROWS, tk), lambda l, j, k: (0, k)),
            pl.BlockSpec((None, tk, tn), lambda l, j, k: (l, k, j)),
            pl.BlockSpec((None, 1, tn), lambda l, j, k: (l, 0, j)),
        ],
        out_specs=pl.BlockSpec((None, N_MOD_ROWS, tn), lambda l, j, k: (l, 0, j)),
        compiler_params=_cparams(("parallel", "parallel", "arbitrary"), 3 * tk * tn * 4),
        name="ada_mod",
    )(cvec, w_ada, b_ada.reshape(DEPTH, 1, n_out))


ROW_TILE = 512


def _stream_specs(x, tm=ROW_TILE, tn=D_MODEL, row_block0=0):
    col = (lambda idx: idx[1]) if tn != D_MODEL else (lambda idx: 0)
    if not isinstance(x, tuple):
        return [pl.BlockSpec((tm, tn), lambda *idx: (row_block0 + idx[0], col(idx)))], [x]
    assert row_block0 == 0
    pb = N_PROMPT // tm
    last_col = D_MODEL // tn - 1
    return ([pl.BlockSpec((tm, tn), lambda *idx: (jnp.minimum(idx[0], pb - 1),
                                                  jnp.where(idx[0] < pb, col(idx), last_col))),
             pl.BlockSpec((tm, tn), lambda *idx: (jnp.maximum(idx[0] - pb, 0),
                                                  jnp.where(idx[0] < pb, 0, col(idx))))], list(x))


def _load_stream(x_refs, tm=ROW_TILE):
    if len(x_refs) == 1:
        return x_refs[0][...]
    in_prompt = pl.program_id(0) < N_PROMPT // tm
    return jnp.where(in_prompt, x_refs[0][...], x_refs[1][...])


def _modulated_norm(x, g_ref, mod_ref, shift_idx, scale_idx):
    y = _rms(x, g_ref[...], D_MODEL)
    sc = mod_ref[0, scale_idx:scale_idx + 1, :]
    sh = mod_ref[0, shift_idx:shift_idx + 1, :]
    return (y * (1.0 + sc) + sh).astype(BF16)


def _normmod_kernel(*refs, n_x, shift_idx, scale_idx):
    g_ref, mod_ref, o_ref = refs[n_x:]
    o_ref[...] = _modulated_norm(_load_stream(refs[:n_x]), g_ref, mod_ref, shift_idx, scale_idx)


def _normmod(x, g, mod, shift_idx, scale_idx):
    x_specs, x_args = _stream_specs(x)
    row = pl.BlockSpec((ROW_TILE, D_MODEL), lambda i: (i, 0))
    return pl.pallas_call(
        functools.partial(_normmod_kernel, n_x=len(x_args), shift_idx=shift_idx, scale_idx=scale_idx),
        out_shape=jax.ShapeDtypeStruct((N_TOK, D_MODEL), BF16),
        grid=(N_TOK // ROW_TILE,),
        in_specs=x_specs + [
            pl.BlockSpec((1, D_MODEL), lambda i: (0, 0)),
            pl.BlockSpec((1, 6, D_MODEL), lambda i: (_mod_row(i, ROW_TILE), 0, 0)),
        ],
        out_specs=row,
        compiler_params=_cparams(("arbitrary",), 8 * ROW_TILE * D_MODEL * 4),
        name="norm_mod",
    )(*x_args, g.reshape(1, D_MODEL), mod)


def _mm_resid_kernel(a_ref, b_ref, *rest, nk, n_x, tm, gate_idx):
    x_refs, mod_refs, o_ref = rest[:n_x], rest[n_x:-1], rest[-1]
    k = pl.program_id(2)

    def prod():
        return jnp.dot(a_ref[...], b_ref[...].astype(BF16), preferred_element_type=F32)

    @pl.when(k == 0)
    def _():
        o_ref[...] = prod()

    @pl.when(jnp.logical_and(k > 0, k < nk - 1))
    def _():
        o_ref[...] += prod()

    @pl.when(k == nk - 1)
    def _():
        upd = o_ref[...] + prod()
        x = _load_stream(x_refs, tm)
        rows = tm // len(mod_refs)
        for part, mod_ref in enumerate(mod_refs):
            sl = slice(part * rows, (part + 1) * rows)
            o_ref[sl, :] = x[sl, :] + mod_ref[0, gate_idx:gate_idx + 1, :] * upd[sl, :]


def _matmul_resid(a, b, layer, x, mod, gate_idx, *, row0=0, rows=N_TOK, name):
    tm, tn, tk = (1024 if isinstance(x, tuple) else 2048), 1024, 1024
    kdim = a.shape[1]
    m = rows
    n = b.shape[-1]
    nk = kdim // tk
    rb0 = row0 // tm
    x_specs, x_args = _stream_specs(x, tm, tn, rb0)
    n_mod = tm // DEC_SEQ
    mod_specs = [pl.BlockSpec((1, 6, tn),
                              lambda i, j, k, p=p: (_mod_row((rb0 + i) * n_mod + p, DEC_SEQ), 0, j))
                 for p in range(n_mod)]
    vmem = (2 * (tm * tk * 2 + tk * tn * b.dtype.itemsize + (1 + len(x_args)) * tm * tn * 4)
            + 2 * tk * tn * 2 + 2 * tm * tk * 2)
    return pl.pallas_call(
        functools.partial(_mm_resid_kernel, nk=nk, n_x=len(x_args), tm=tm, gate_idx=gate_idx),
        out_shape=jax.ShapeDtypeStruct((m, n), F32),
        grid=(m // tm, n // tn, nk),
        in_specs=[pl.BlockSpec((tm, tk), lambda i, j, k: (rb0 + i, k)),
                  pl.BlockSpec((None, tk, tn), lambda i, j, k: (layer, k, j))] + x_specs + mod_specs,
        out_specs=pl.BlockSpec((tm, tn), lambda i, j, k: (i, j)),
        compiler_params=_cparams(("arbitrary", "arbitrary", "arbitrary"), vmem),
        name=name,
    )(a, b, *x_args, *([mod] * n_mod))


def _mm_ws_kernel(a_ref, b_ref, o_ref, *scratch, relu2, b_transposed):
    if scratch:
        bw_ref = scratch[0]

        @pl.when(pl.program_id(1) == 0)
        def _():
            bw_ref[...] = b_ref[...].astype(BF16)

        b = bw_ref[...]
    else:
        b = b_ref[...]
    contract_b = 1 if b_transposed else 0
    r = lax.dot_general(a_ref[...], b, (((1,), (contract_b,)), ((), ())), preferred_element_type=F32)
    if relu2:
        r = jnp.square(jnp.maximum(r, 0.0))
    o_ref[...] = r.astype(o_ref.dtype)


def _matmul_ws(a, b, layer, *, tm, tn, out_dtype, relu2=False, b_transposed=False, name):
    m, kdim = a.shape
    if b_transposed:
        n = b.shape[0]
        b_spec = pl.BlockSpec((tn, kdim), lambda j, i: (j, 0))
    elif layer is None:
        n = b.shape[-1]
        b_spec = pl.BlockSpec((kdim, tn), lambda j, i: (0, j))
    else:
        n = b.shape[-1]
        b_spec = pl.BlockSpec((None, kdim, tn), lambda j, i: (layer, 0, j))
    cast = b.dtype != jnp.dtype(BF16)
    o_bytes = jnp.dtype(out_dtype).itemsize
    vmem = (2 * (tm * kdim * 2 + kdim * tn * b.dtype.itemsize + tm * tn * o_bytes)
            + cast * kdim * tn * 2 + 2 * tm * tn * 4)
    return pl.pallas_call(
        functools.partial(_mm_ws_kernel, relu2=relu2, b_transposed=b_transposed),
        out_shape=jax.ShapeDtypeStruct((m, n), out_dtype),
        grid=(n // tn, m // tm),
        in_specs=[pl.BlockSpec((tm, kdim), lambda j, i: (i, 0)), b_spec],
        out_specs=pl.BlockSpec((tm, tn), lambda j, i: (i, j)),
        scratch_shapes=[pltpu.VMEM((tn, kdim) if b_transposed else (kdim, tn), BF16)] if cast else [],
        compiler_params=_cparams(("parallel", "arbitrary"), vmem),
        name=name,
    )(a, b)


def _rope128(r, cos, sa, sb):
    return r * cos + pltpu.roll(r, 96, 1) * sa + pltpu.roll(r, 32, 1) * sb


def _q_kernel(ql_ref, ga_ref, w_ref, gn_ref, gr_ref, cos_ref, sa_ref, sb_ref, o_ref, qn_ref, *, heads_per_tile):
    @pl.when(pl.program_id(1) == 0)
    def _():
        qn_ref[...] = _rms(ql_ref[...], ga_ref[...], Q_LORA).astype(BF16)

    cos, sa, sb = cos_ref[...], sa_ref[...], sb_ref[...]
    qn = qn_ref[...]
    for hh in range(heads_per_tile):
        c0 = hh * QK_PAD
        acc = jnp.dot(qn, w_ref[:, c0:c0 + QK_PAD], preferred_element_type=F32)
        nope = acc[:, :QK_NOPE]
        rope = acc[:, QK_NOPE:]
        nope = _rms(nope, gn_ref[...], QK_NOPE) * ATTN_SCALE
        rope = _rope128(_rms(rope, gr_ref[...], QK_ROPE), cos, sa, sb) * ATTN_SCALE
        o_ref[:, c0:c0 + QK_NOPE] = nope.astype(BF16)
        o_ref[:, c0 + QK_NOPE:c0 + QK_PAD] = rope.astype(BF16)


def _rope_specs(tm, nargs):
    blocks_per_seq = DEC_SEQ // tm

    def idx(i, *_):
        return (jnp.minimum(i * tm // N_PROMPT, 1), i % blocks_per_seq, 0)

    return [pl.BlockSpec((None, tm, V7X_LANES), idx) for _ in range(nargs)]


def _q_proj(proj, ga, w_qb_p, gn, gr128, rope_tabs):
    tm, tn = 512, 1024
    n = MLA_HEADS * QK_PAD
    return pl.pallas_call(
        functools.partial(_q_kernel, heads_per_tile=tn // QK_PAD),
        out_shape=jax.ShapeDtypeStruct((N_TOK, n), BF16),
        grid=(N_TOK // tm, n // tn),
        in_specs=[
            pl.BlockSpec((tm, Q_LORA), lambda i, j: (i, 0)),
            pl.BlockSpec((1, Q_LORA), lambda i, j: (0, 0)),
            pl.BlockSpec((Q_LORA, tn), lambda i, j: (0, j)),
            pl.BlockSpec((1, QK_NOPE), lambda i, j: (0, 0)),
            pl.BlockSpec((1, V7X_LANES), lambda i, j: (0, 0)),
        ] + _rope_specs(tm, 3),
        out_specs=pl.BlockSpec((tm, tn), lambda i, j: (i, j)),
        scratch_shapes=[pltpu.VMEM((tm, Q_LORA), BF16)],
        compiler_params=_cparams(
            ("parallel", "arbitrary"),
            2 * (tm * Q_LORA * 4 + Q_LORA * tn * 2 + tm * tn * 2) + tm * Q_LORA * 2 + 3 * tm * tn * 4),
        name="mla_q_proj",
    )(proj, ga, w_qb_p, gn, gr128, *rope_tabs)


def _kvnorm_kernel(c_ref, r_ref, gc_ref, gk_ref, cos_ref, sa_ref, sb_ref, ckv_ref, kr_ref, krr_ref):
    ckv_ref[...] = _rms(c_ref[...], gc_ref[...], KV_LORA)
    rn = _rms(r_ref[...], gk_ref[...], QK_ROPE)
    kr_ref[...] = rn
    krr_ref[...] = _rope128(rn, cos_ref[...], sa_ref[...], sb_ref[...]).astype(BF16)


def _kv_norm(proj, gc, gk128, rope_tabs):
    tm = 512
    return pl.pallas_call(
        _kvnorm_kernel,
        out_shape=(jax.ShapeDtypeStruct((N_TOK, KV_LORA), F32),
                   jax.ShapeDtypeStruct((N_TOK, V7X_LANES), F32),
                   jax.ShapeDtypeStruct((N_TOK, V7X_LANES), BF16)),
        grid=(N_TOK // tm,),
        in_specs=[
            pl.BlockSpec((tm, KV_LORA), lambda i: (i, P_KV // KV_LORA)),
            pl.BlockSpec((tm, V7X_LANES), lambda i: (i, P_KROPE // V7X_LANES)),
            pl.BlockSpec((1, KV_LORA), lambda i: (0, 0)),
            pl.BlockSpec((1, V7X_LANES), lambda i: (0, 0)),
        ] + _rope_specs(tm, 3),
        out_specs=(pl.BlockSpec((tm, KV_LORA), lambda i: (i, 0)),
                   pl.BlockSpec((tm, V7X_LANES), lambda i: (i, 0)),
                   pl.BlockSpec((tm, V7X_LANES), lambda i: (i, 0))),
        compiler_params=_cparams(("parallel",), 8 * tm * (KV_LORA + 4 * V7X_LANES) * 4),
        name="mla_kv_norm",
    )(proj, proj, gc, gk128, *rope_tabs)


def _kvup_kernel(own_ref, cache_ref, w_ref, gn_ref, o_ref, *, heads_per_tile, k_tiles, own_blocks):
    c = jnp.where(pl.program_id(0) < own_blocks, own_ref[...], cache_ref[...]).astype(BF16)
    pair = 2 * QK_NOPE

    @pl.when(pl.program_id(1) < k_tiles)
    def _():
        for pp in range(heads_per_tile // 2):
            acc = jnp.dot(c, w_ref[:, pp * pair:(pp + 1) * pair], preferred_element_type=F32)
            for hh in range(2):
                c0 = pp * pair + hh * QK_NOPE
                o_ref[:, c0:c0 + QK_NOPE] = _rms(
                    acc[:, hh * QK_NOPE:(hh + 1) * QK_NOPE], gn_ref[...], QK_NOPE).astype(BF16)

    @pl.when(pl.program_id(1) >= k_tiles)
    def _():
        o_ref[...] = jnp.dot(c, w_ref[...], preferred_element_type=F32).astype(BF16)


def _kv_up(ckv, cache, w_kvb_p, gn):
    tm, tn = 512, 1024
    own_blocks = N_TOK // tm
    m = N_TOK + N_CACHE
    n = 2 * MLA_OUT
    return pl.pallas_call(
        functools.partial(_kvup_kernel, heads_per_tile=tn // QK_NOPE, k_tiles=MLA_OUT // tn,
                          own_blocks=own_blocks),
        out_shape=jax.ShapeDtypeStruct((m, n), BF16),
        grid=(m // tm, n // tn),
        in_specs=[
            pl.BlockSpec((tm, KV_LORA), lambda i, j: (jnp.minimum(i, own_blocks - 1), 0)),
            pl.BlockSpec((tm, KV_LORA), lambda i, j: (jnp.maximum(i - own_blocks, 0), 0)),
            pl.BlockSpec((KV_LORA, tn), lambda i, j: (0, j)),
            pl.BlockSpec((1, QK_NOPE), lambda i, j: (0, 0)),
        ],
        out_specs=pl.BlockSpec((tm, tn), lambda i, j: (i, j)),
        compiler_params=_cparams(
            ("arbitrary", "arbitrary"),
            2 * (2 * tm * KV_LORA * 4 + KV_LORA * tn * 2 + tm * tn * 2) + 3 * tm * tn * 4),
        name="mla_kv_up",
    )(ckv, cache, w_kvb_p, gn)


def _attn_kernel(*refs, n_seg):
    q_ref = refs[0]
    seg_refs = [refs[1 + 3 * s:4 + 3 * s] for s in range(n_seg)]
    g_ref = refs[1 + 3 * n_seg]
    o_ref, oacc_ref = refs[-2:]
    def head_scores(h):
        qh = q_ref[:, h * QK_PAD:(h + 1) * QK_PAD]
        out = []
        for kn_ref, kr_ref, _ in seg_refs:
            kh = jnp.concatenate([kn_ref[:, h * QK_NOPE:(h + 1) * QK_NOPE], kr_ref[...]], axis=1)
            out.append(lax.dot_general(qh, kh, (((1,), (1,)), ((), ())), preferred_element_type=F32))
        return out

    next_scores = head_scores(0)
    for h in range(MLA_HEADS):
        scores = next_scores
        if h + 1 < MLA_HEADS:
            next_scores = head_scores(h + 1)
        m = scores[0].max(axis=-1, keepdims=True)
        for s in scores[1:]:
            m = jnp.maximum(m, s.max(axis=-1, keepdims=True))
        denom = None
        out = None
        for s, (_, _, v_ref) in zip(scores, seg_refs):
            p = jnp.exp(s - m)
            ps = p.sum(axis=-1, keepdims=True)
            pv = jnp.dot(p.astype(BF16), v_ref[:, h * V_HEAD:(h + 1) * V_HEAD], preferred_element_type=F32)
            denom = ps if denom is None else denom + ps
            out = pv if out is None else out + pv
        oacc_ref[:, h * V_HEAD:(h + 1) * V_HEAD] = out / denom
    o_ref[...] = _rms(oacc_ref[...], g_ref[...], MLA_OUT).astype(BF16)


def _mix_alias(mix, args, in_specs, out_idx=0):
    if mix is None:
        return {}
    args.append(mix)
    in_specs.append(pl.BlockSpec(memory_space=pl.ANY))
    return {len(args) - 1: out_idx}


def _attention(q, kv, kr, g_out, mix, *, n_batch, t_q, segs, q_row0):
    tq = 256
    qb = t_q // tq
    in_specs = [pl.BlockSpec((tq, MLA_HEADS * QK_PAD), lambda b, i: (q_row0 // tq + b * qb + i, 0))]
    args = [q]
    vmem = 2 * tq * MLA_HEADS * QK_PAD * 2
    for rows, row0 in segs:
        blk0 = row0 // rows
        in_specs += [
            pl.BlockSpec((rows, MLA_OUT), lambda b, i, blk0=blk0: (blk0 + b, 0)),
            pl.BlockSpec((rows, V7X_LANES), lambda b, i, blk0=blk0: (blk0 + b, 0)),
            pl.BlockSpec((rows, MLA_OUT), lambda b, i, blk0=blk0: (blk0 + b, 1)),
        ]
        args += [kv, kr, kv]
        vmem += 2 * rows * (2 * MLA_OUT + V7X_LANES) * 2 + 12 * tq * rows * 4
    in_specs.append(pl.BlockSpec((1, MLA_OUT), lambda b, i: (0, 0)))
    args.append(g_out)
    vmem += 4 * tq * MLA_OUT * 4
    aliases = _mix_alias(mix, args, in_specs)
    return pl.pallas_call(
        functools.partial(_attn_kernel, n_seg=len(segs)),
        out_shape=jax.ShapeDtypeStruct((N_TOK, MLA_OUT + GLA_OUT), BF16),
        grid=(n_batch, qb),
        in_specs=in_specs,
        out_specs=pl.BlockSpec((tq, MLA_OUT), lambda b, i: (q_row0 // tq + b * qb + i, 0)),
        scratch_shapes=[pltpu.VMEM((tq, MLA_OUT), F32)],
        input_output_aliases=aliases,
        compiler_params=_cparams(("parallel", "arbitrary"), vmem),
        name="mla_attention_%d" % t_q,
    )(*args)


def _log_sigmoid(x):
    return -(jnp.maximum(-x, 0.0) + jnp.log(1.0 + jnp.exp(-jnp.abs(x))))


def _exact_tri_dot(keep, g):
    tri = jnp.where(keep, 1.0, 0.0).astype(BF16)
    g_hi = g.astype(BF16)
    rem = g - g_hi.astype(F32)
    g_mid = rem.astype(BF16)
    g_lo = (rem - g_mid.astype(F32)).astype(BF16)
    return (jnp.dot(tri, g_hi, preferred_element_type=F32)
            + jnp.dot(tri, g_mid, preferred_element_type=F32)
            + jnp.dot(tri, g_lo, preferred_element_type=F32))


def _gla_head_group(t):
    return GLA_HEADS if t <= SEQ else GLA_HEADS // 2


def _gla_kernel(q_ref, k_ref, v_ref, glr_ref, wgf_ref, wgb_ref, bgf_ref, bgb_ref, s0f_ref, s0b_ref,
                og_ref, gn_ref, *rest, n_chunks, hg):
    o_ref, sf_ref, sb_ref, gf_s, gb_s, of_s, ob_s, st_s = rest[-8:]
    glr = glr_ref[...].astype(BF16)

    def gate(w_ref, b_ref):
        x = jnp.dot(glr, w_ref[...].astype(BF16), preferred_element_type=F32) + b_ref[...]
        return _log_sigmoid(x) * (1.0 / GATE_TAU)

    gf_s[...] = gate(wgf_ref, bgf_ref)
    gb_s[...] = gate(wgb_ref, bgb_ref)
    for j in range(hg):
        st_s[j] = s0f_ref[j].T
        st_s[hg + j] = s0b_ref[j].T

    row = lax.broadcasted_iota(jnp.int32, (CHUNK, CHUNK), 0)
    col = lax.broadcasted_iota(jnp.int32, (CHUNK, CHUNK), 1)
    nt = (((1,), (1,)), ((), ()))
    tn = (((0,), (0,)), ((), ()))

    keep_f = col <= row
    keep_b = col >= row

    def cumulate(n, carry):
        sl = pl.ds(pl.multiple_of(n * CHUNK, CHUNK), CHUNK)
        gf_s[sl, :] = _exact_tri_dot(keep_f, gf_s[sl, :])
        gb_s[sl, :] = _exact_tri_dot(keep_b, gb_s[sl, :])
        return carry

    lax.fori_loop(0, n_chunks, cumulate, 0)

    def step(n, carry):
        units = []
        for j in range(hg):
            units.append((n, j, j, gf_s, of_s, keep_f, CHUNK - 1, CHUNK // 2 - 1))
            units.append((n_chunks - 1 - n, j, hg + j, gb_s, ob_s, keep_b, 0, CHUNK // 2))
        staged = []
        for cn, j, slot, b_s, o_s, keep, end_idx, mid_idx in units:
            sl = pl.ds(pl.multiple_of(cn * CHUNK, CHUNK), CHUNK)
            dk = slice(j * GLA_DK, (j + 1) * GLA_DK)
            dv = slice(j * GLA_DV, (j + 1) * GLA_DV)
            b = b_s[sl, dk]
            q = q_ref[sl, dk] * GLA_SCALE
            k = k_ref[sl, dk]
            v = v_ref[sl, dv].astype(BF16)
            b_end = b[end_idx:end_idx + 1, :]
            b_mid = b[mid_idx:mid_idx + 1, :]
            qe = (q * jnp.exp(b)).astype(BF16)
            qa = (q * jnp.exp(b - b_mid)).astype(BF16)
            ka = (k * jnp.exp(b_mid - b)).astype(BF16)
            kd = (k * jnp.exp(b_end - b)).astype(BF16)
            staged.append((sl, dv, slot, o_s, keep, v, qe, qa, ka, kd, jnp.exp(b_end)))
        scores = [lax.dot_general(qa, ka, nt, preferred_element_type=F32)
                  for _, _, _, _, _, _, _, qa, ka, _, _ in staged]
        inter = [lax.dot_general(qe, st_s[slot].astype(BF16), nt, preferred_element_type=F32)
                 for _, _, slot, _, _, _, qe, _, _, _, _ in staged]
        update = [lax.dot_general(v, kd, tn, preferred_element_type=F32)
                  for _, _, _, _, _, v, _, _, _, kd, _ in staged]
        for (sl, dv, slot, o_s, keep, v, _, _, _, _, decay), a, o_inter, upd in zip(staged, scores, inter, update):
            a = jnp.where(keep, a, 0.0).astype(BF16)
            o_s[sl, dv] = jnp.dot(a, v, preferred_element_type=F32) + o_inter
            st_s[slot] = st_s[slot] * decay + upd
        return carry

    lax.fori_loop(0, n_chunks, step, 0)
    for j in range(hg):
        sf_ref[j] = st_s[j].T
        sb_ref[j] = st_s[hg + j].T
        dv = slice(j * GLA_DV, (j + 1) * GLA_DV)
        og = og_ref[:, dv]
        o = _rms(of_s[:, dv] + ob_s[:, dv], gn_ref[...], GLA_DV)
        o_ref[:, dv] = (o * (og / (1.0 + jnp.exp(-og)))).astype(BF16)


def _gla(proj, wg, bg, s0f, s0b, gn, mix, *, n_batch, t, row0, layer=0, n_layers=1, states=(None, None)):
    hg = _gla_head_group(t)
    rb0 = row0 // t
    n_groups = GLA_HEADS // hg
    tok = lambda width, col0: pl.BlockSpec(
        (t, hg * width), lambda b, h, cb=col0 // (hg * width): (rb0 + b, cb + h))
    state = pl.BlockSpec((None, hg, GLA_DK, GLA_DV), lambda b, h: (b, h, 0, 0))
    in_specs = [
        tok(GLA_DK, P_GQ),
        tok(GLA_DK, P_GK),
        tok(GLA_DV, P_GV),
        pl.BlockSpec((t, V7X_LANES), lambda b, h: (rb0 + b, P_GATE // V7X_LANES)),
        pl.BlockSpec((V7X_LANES, hg * GLA_DK), lambda b, h: (0, h)),
        pl.BlockSpec((V7X_LANES, hg * GLA_DK), lambda b, h: (0, n_groups + h)),
        pl.BlockSpec((1, hg * GLA_DK), lambda b, h: (0, h)),
        pl.BlockSpec((1, hg * GLA_DK), lambda b, h: (0, n_groups + h)),
        state, state,
        tok(GLA_DV, P_OG),
        pl.BlockSpec((1, GLA_DV), lambda b, h: (0, 0)),
    ]
    args = [proj, proj, proj, proj, wg, wg, bg, bg, s0f, s0b, proj, gn]
    aliases = _mix_alias(mix, args, in_specs)
    aliases.update(_mix_alias(states[0], args, in_specs, 1))
    aliases.update(_mix_alias(states[1], args, in_specs, 2))
    state_out = pl.BlockSpec((None, None, hg, GLA_DK, GLA_DV), lambda b, h: (b, layer, h, 0, 0))
    state_shape = jax.ShapeDtypeStruct((n_batch, n_layers, GLA_HEADS, GLA_DK, GLA_DV), F32)
    return pl.pallas_call(
        functools.partial(_gla_kernel, n_chunks=t // CHUNK, hg=hg),
        out_shape=(jax.ShapeDtypeStruct((N_TOK, MLA_OUT + GLA_OUT), BF16), state_shape, state_shape),
        grid=(n_batch, n_groups),
        in_specs=in_specs,
        input_output_aliases=aliases,
        out_specs=(pl.BlockSpec((t, hg * GLA_DV),
                                lambda b, h: (rb0 + b, MLA_OUT // (hg * GLA_DV) + h)), state_out, state_out),
        scratch_shapes=[pltpu.VMEM((t, hg * GLA_DK), F32), pltpu.VMEM((t, hg * GLA_DK), F32),
                        pltpu.VMEM((t, hg * GLA_DV), F32), pltpu.VMEM((t, hg * GLA_DV), F32),
                        pltpu.VMEM((2 * hg, GLA_DV, GLA_DK), F32)],
        compiler_params=_cparams(
            ("parallel", "parallel"),
            2 * (t * hg * (2 * GLA_DK + 2 * GLA_DV) * 4 + t * V7X_LANES * 4 + t * hg * GLA_DV * 2)
            + t * hg * (2 * GLA_DK + 2 * GLA_DV) * 4 + 6 * t * hg * GLA_DK * 4
            + 10 * hg * GLA_DK * GLA_DV * 4 + 4 * V7X_LANES * hg * GLA_DK * 2),
        name="gla_%d" % t,
    )(*args)


def _pack_w_in_kernel(w_ref, o_ref):
    segments = ((0, 0, OFF_KV + KV_LORA), (P_KROPE, OFF_KV + KV_LORA, QK_ROPE), (P_GATE, OFF_GATE, 2 * GATE_RANK),
                (P_GQ, OFF_GQ, OFF_GATE - OFF_GQ), (P_OG, OFF_OG, GLA_OUT))
    cols = o_ref.shape[1]
    o_ref[P_KROPE:P_GQ, :] = jnp.zeros((P_GQ - P_KROPE, cols), BF16)
    for dst, src, rows in segments:
        o_ref[dst:dst + rows, :] = w_ref[src:src + rows, :].astype(BF16)


def _pack_w_in(w_in_t, layer):
    tc = 256
    return pl.pallas_call(
        _pack_w_in_kernel,
        out_shape=jax.ShapeDtypeStruct((P_TOTAL, D_MODEL), BF16),
        grid=(D_MODEL // tc,),
        in_specs=[pl.BlockSpec((None, N_IN, tc), lambda i: (layer, 0, i))],
        out_specs=pl.BlockSpec((P_TOTAL, tc), lambda i: (0, i)),
        compiler_params=_cparams(("parallel",), 2 * tc * (N_IN * 4 + P_TOTAL * 2) + 2 * tc * P_TOTAL * 4),
        name="pack_w_in",
    )(w_in_t)


def _pack_w_qb(w):
    w = w.reshape(Q_LORA, MLA_HEADS, QK_NOPE + QK_ROPE)
    w = jnp.pad(w, ((0, 0), (0, 0), (0, QK_PAD - QK_NOPE - QK_ROPE)))
    return w.reshape(Q_LORA, MLA_HEADS * QK_PAD).astype(BF16)


def _pack_w_kvb(w):
    w = w.reshape(KV_LORA, MLA_HEADS, QK_NOPE + V_HEAD)
    return jnp.concatenate([w[:, :, :QK_NOPE].reshape(KV_LORA, MLA_OUT),
                            w[:, :, QK_NOPE:].reshape(KV_LORA, MLA_OUT)], axis=1).astype(BF16)


def _pack_gate(w_gf2, w_gb2, b_gf, b_gb):
    wg = jnp.zeros((V7X_LANES, 2 * GLA_QK), F32)
    wg = wg.at[:GATE_RANK, :GLA_QK].set(w_gf2).at[GATE_RANK:2 * GATE_RANK, GLA_QK:].set(w_gb2)
    return wg.astype(BF16), jnp.concatenate([b_gf, b_gb]).reshape(1, 2 * GLA_QK)


def _pad_lanes(g):
    return jnp.pad(g, (0, V7X_LANES - g.shape[0])).reshape(1, V7X_LANES)


def _rope_tables():
    rows = DEC_SEQ // GRID_W
    row = jnp.repeat(jnp.arange(rows, dtype=F32), GRID_W)
    col = jnp.tile(jnp.arange(GRID_W, dtype=F32), rows)
    n_freq = QK_ROPE // 4
    inv = jnp.power(ROPE_BASE, -jnp.arange(n_freq, dtype=F32) / n_freq)
    ang = jnp.concatenate([row[:, None] * inv, col[:, None] * inv], axis=-1)
    cos, sin = jnp.cos(ang), jnp.sin(ang)
    half = QK_ROPE // 2
    zpad = jnp.zeros((DEC_SEQ, V7X_LANES - QK_ROPE), F32)
    zhalf = jnp.zeros((DEC_SEQ, half), F32)
    cos_t = jnp.concatenate([cos, cos, zpad], axis=1)
    sa_t = jnp.concatenate([-sin, zhalf, zpad], axis=1)
    sb_t = jnp.concatenate([zhalf, sin, zpad], axis=1)
    ident = jnp.concatenate([jnp.ones((DEC_SEQ, QK_ROPE), F32), zpad], axis=1)
    zeros = jnp.zeros((DEC_SEQ, V7X_LANES), F32)
    return (jnp.stack([ident, cos_t]), jnp.stack([zeros, sa_t]), jnp.stack([zeros, sb_t]))


def kernel(x_prompt, x_sample, c, cache_ckv, cache_krope, state_gla_fwd, state_gla_bwd, c_ctx, w_ada, b_ada, norm1, norm2, w_in, q_a_norm, w_qb, kv_a_norm, w_kvb, q_norm_nope, q_norm_rope, k_norm_nope, k_norm_rope, w_gf2, b_gf, w_gb2, b_gb, gla_norm, mla_out_norm, w_o, w_up, w_down):
    x = (x_prompt.reshape(N_PROMPT, D_MODEL), x_sample.reshape(N_SAMPLE, D_MODEL))
    cvec = jnp.concatenate([c_ctx[None, :], c, jnp.zeros((N_MOD_ROWS - 1 - DEC_BATCH, D_MODEL), F32)], axis=0)
    mods = _ada(cvec, w_ada, b_ada).reshape(DEPTH, N_MOD_ROWS, 6, D_MODEL)
    rope_tabs = _rope_tables()
    zero_state = jnp.zeros((BATCH, GLA_HEADS, GLA_DK, GLA_DV), F32)
    w_in_t = jnp.swapaxes(w_in, 1, 2)

    ckvs, kropes = [], []
    state_stacks = (None, None)
    h = _normmod(x, norm1[0], mods[0], 0, 1)
    for l in range(DEPTH):
        mod = mods[l]
        proj = _matmul_ws(h, _pack_w_in(w_in_t, l), None, tm=1024, tn=1024, out_dtype=F32,
                          b_transposed=True, name="in_proj")

        q = _q_proj(proj, q_a_norm[l].reshape(1, Q_LORA), _pack_w_qb(w_qb[l]),
                    q_norm_nope[l].reshape(1, QK_NOPE), _pad_lanes(q_norm_rope[l]), rope_tabs)
        ckv, kr, krr = _kv_norm(proj, kv_a_norm[l].reshape(1, KV_LORA), _pad_lanes(k_norm_rope[l]), rope_tabs)
        kv = _kv_up(ckv, cache_ckv[:, l].reshape(N_CACHE, KV_LORA), _pack_w_kvb(w_kvb[l]),
                    k_norm_nope[l].reshape(1, QK_NOPE))
        kr_cache = jnp.pad(cache_krope[:, l].reshape(N_CACHE, QK_ROPE),
                           ((0, 0), (0, V7X_LANES - QK_ROPE))).astype(BF16)
        kr_all = jnp.concatenate([krr, kr_cache], axis=0)
        g_mla = mla_out_norm[l].reshape(1, MLA_OUT)
        mix = _attention(q, kv, kr_all, g_mla, None, n_batch=BATCH, t_q=SEQ, q_row0=0,
                         segs=[(SEQ, 0)])
        mix = _attention(q, kv, kr_all, g_mla, mix, n_batch=DEC_BATCH, t_q=DEC_SEQ, q_row0=N_PROMPT,
                         segs=[(DEC_SEQ, N_PROMPT), (PAST_LEN, N_TOK)])

        wg, bg = _pack_gate(w_gf2[l], w_gb2[l], b_gf[l], b_gb[l])
        gn = gla_norm[l].reshape(1, GLA_DV)
        mix, *state_stacks = _gla(proj, wg, bg, zero_state, zero_state, gn, mix, n_batch=BATCH, t=SEQ, row0=0,
                                  layer=l, n_layers=DEPTH, states=state_stacks)
        mix, _, _ = _gla(proj, wg, bg, state_gla_fwd[:, l], state_gla_bwd[:, l], gn, mix,
                         n_batch=DEC_BATCH, t=DEC_SEQ, row0=N_PROMPT)

        x = _matmul_resid(mix, w_o, l, x, mod, 2, name="out_proj")

        h2 = _normmod(x, norm2[l], mod, 3, 4)
        act = _matmul_ws(h2, w_up, l, tm=512, tn=1024, out_dtype=BF16, relu2=True, name="mlp_up")
        if l + 1 < DEPTH:
            x = _matmul_resid(act, w_down, l, x, mod, 5, name="mlp_down")
            h = _normmod(x, norm1[l + 1], mods[l + 1], 0, 1)
        else:
            y_prompt = _matmul_resid(act, w_down, l, x, mod, 5, row0=0, rows=N_PROMPT, name="mlp_down_ctx")
            y_sample = _matmul_resid(act, w_down, l, x, mod, 5, row0=N_PROMPT, rows=N_SAMPLE, name="mlp_down_lat")

        ckvs.append(ckv[:N_PROMPT].reshape(BATCH, SEQ, KV_LORA))
        kropes.append(kr[:N_PROMPT, :QK_ROPE].reshape(BATCH, SEQ, QK_ROPE))

    return (y_prompt.reshape(BATCH, SEQ, D_MODEL),
            y_sample.reshape(DEC_BATCH, DEC_SEQ, D_MODEL),
            jnp.stack(ckvs, axis=1), jnp.stack(kropes, axis=1),
            state_stacks[0], state_stacks[1])
```

```python
import functools

import jax
import jax.numpy as jnp
from jax import lax
from jax.experimental import pallas as pl
from jax.experimental.pallas import tpu as pltpu

D_MODEL = 4096
BATCH = 16
SEQ = 256
DEPTH = 2
DEC_BATCH = 4
DEC_SEQ = 1024
PAST_LEN = 512
GRID_W = 64
MLA_HEADS = 16
QK_NOPE = 128
QK_ROPE = 64
V_HEAD = 128
Q_LORA = 1024
KV_LORA = 512
GLA_HEADS = 8
GLA_DK = 128
GLA_DV = 256
GATE_RANK = 16
GATE_TAU = 16.0
CHUNK = 64
D_FF = 4 * D_MODEL
ROPE_BASE = 10000.0
EPS = 1e-6

MLA_OUT = MLA_HEADS * V_HEAD
GLA_QK = GLA_HEADS * GLA_DK
GLA_OUT = GLA_HEADS * GLA_DV
OFF_KV = Q_LORA
OFF_GQ = OFF_KV + KV_LORA + QK_ROPE
OFF_GK = OFF_GQ + GLA_QK
OFF_GV = OFF_GK + GLA_QK
OFF_GATE = OFF_GV + GLA_OUT
OFF_OG = OFF_GATE + 2 * GATE_RANK
N_IN = OFF_OG + GLA_OUT

F32 = jnp.float32
BF16 = jnp.bfloat16

V7X_LANES = 128
V7X_SCOPED_VMEM_BYTES = 60000 * 1024

N_PROMPT = BATCH * SEQ
N_SAMPLE = DEC_BATCH * DEC_SEQ
N_TOK = N_PROMPT + N_SAMPLE
N_CACHE = DEC_BATCH * PAST_LEN
N_MOD_ROWS = 8

P_KV = 1024
P_KROPE = P_KV + KV_LORA
P_GATE = P_KROPE + V7X_LANES
P_GQ = 2048
P_GK = P_GQ + GLA_QK
P_GV = P_GK + GLA_QK
P_OG = P_GV + GLA_OUT
P_TOTAL = P_OG + GLA_OUT
QK_PAD = 2 * V7X_LANES

ATTN_SCALE = float((QK_NOPE + QK_ROPE) ** -0.5)
GLA_SCALE = float(GLA_DK ** -0.5)


def _cparams(sem, vmem_bytes):
    return pltpu.CompilerParams(dimension_semantics=sem,
                                vmem_limit_bytes=min(int(vmem_bytes), V7X_SCOPED_VMEM_BYTES))


def _mod_row(row_block, rows_per_block):
    return jnp.maximum(row_block * rows_per_block // DEC_SEQ - (N_PROMPT // DEC_SEQ - 1), 0)


def _rms(x, g, n):
    ms = jnp.sum(x * x, axis=-1, keepdims=True) * (1.0 / n)
    return x * lax.rsqrt(ms + EPS) * g


def _ada_kernel(c_ref, w_ref, b_ref, o_ref):
    k = pl.program_id(2)

    @pl.when(k == 0)
    def _():
        o_ref[...] = jnp.broadcast_to(b_ref[...], o_ref.shape)

    c = c_ref[...]
    s = (c / (1.0 + jnp.exp(-c))).astype(BF16)
    o_ref[...] += jnp.dot(s, w_ref[...].astype(BF16), preferred_element_type=F32)


def _ada(cvec, w_ada, b_ada):
    tk, tn = 1024, 2048
    n_out = 6 * D_MODEL
    return pl.pallas_call(
        _ada_kernel,
        out_shape=jax.ShapeDtypeStruct((DEPTH, N_MOD_ROWS, n_out), F32),
        grid=(DEPTH, n_out // tn, D_MODEL // tk),
        in_specs=[
            pl.BlockSpec((N_MOD_ROWS, tk), lambda l, j, k: (0, k)),
            pl.BlockSpec((None, tk, tn), lambda l, j, k: (l, k, j)),
            pl.BlockSpec((None, 1, tn), lambda l, j, k: (l, 0, j)),
        ],
        out_specs=pl.BlockSpec((None, N_MOD_ROWS, tn), lambda l, j, k: (l, 0, j)),
        compiler_params=_cparams(("parallel", "parallel", "arbitrary"), 3 * tk * tn * 4),
        name="ada_mod",
    )(cvec, w_ada, b_ada.reshape(DEPTH, 1, n_out))


ROW_TILE = 512


def _stream_specs(x):
    if not isinstance(x, tuple):
        return [pl.BlockSpec((ROW_TILE, D_MODEL), lambda i: (i, 0))], [x]
    pb = N_PROMPT // ROW_TILE
    return ([pl.BlockSpec((ROW_TILE, D_MODEL), lambda i: (jnp.minimum(i, pb - 1), 0)),
             pl.BlockSpec((ROW_TILE, D_MODEL), lambda i: (jnp.maximum(i - pb, 0), 0))], list(x))


def _load_stream(x_refs):
    if len(x_refs) == 1:
        return x_refs[0][...]
    in_prompt = pl.program_id(0) < N_PROMPT // ROW_TILE
    return jnp.where(in_prompt, x_refs[0][...], x_refs[1][...])


def _modulated_norm(x, g_ref, mod_ref, shift_idx, scale_idx):
    y = _rms(x, g_ref[...], D_MODEL)
    sc = mod_ref[0, scale_idx:scale_idx + 1, :]
    sh = mod_ref[0, shift_idx:shift_idx + 1, :]
    return (y * (1.0 + sc) + sh).astype(BF16)


def _normmod_kernel(*refs, n_x, shift_idx, scale_idx):
    g_ref, mod_ref, o_ref = refs[n_x:]
    o_ref[...] = _modulated_norm(_load_stream(refs[:n_x]), g_ref, mod_ref, shift_idx, scale_idx)


def _normmod(x, g, mod, shift_idx, scale_idx):
    x_specs, x_args = _stream_specs(x)
    row = pl.BlockSpec((ROW_TILE, D_MODEL), lambda i: (i, 0))
    return pl.pallas_call(
        functools.partial(_normmod_kernel, n_x=len(x_args), shift_idx=shift_idx, scale_idx=scale_idx),
        out_shape=jax.ShapeDtypeStruct((N_TOK, D_MODEL), BF16),
        grid=(N_TOK // ROW_TILE,),
        in_specs=x_specs + [
            pl.BlockSpec((1, D_MODEL), lambda i: (0, 0)),
            pl.BlockSpec((1, 6, D_MODEL), lambda i: (_mod_row(i, ROW_TILE), 0, 0)),
        ],
        out_specs=row,
        compiler_params=_cparams(("arbitrary",), 8 * ROW_TILE * D_MODEL * 4),
        name="norm_mod",
    )(*x_args, g.reshape(1, D_MODEL), mod)


def _mm_resid_kernel(a_ref, b_ref, x_ref, *rest, nk, n_mod, gate_idx):
    mod_refs, o_ref = rest[:n_mod], rest[-1]
    k = pl.program_id(2)

    def prod():
        return jnp.dot(a_ref[...], b_ref[...].astype(BF16), preferred_element_type=F32)

    @pl.when(k == 0)
    def _():
        o_ref[...] = prod()

    @pl.when(jnp.logical_and(k > 0, k < nk - 1))
    def _():
        o_ref[...] += prod()

    @pl.when(k == nk - 1)
    def _():
        upd = o_ref[...] + prod()
        x = x_ref[...]
        rows = o_ref.shape[0] // n_mod
        for part, mod_ref in enumerate(mod_refs):
            sl = slice(part * rows, (part + 1) * rows)
            o_ref[sl, :] = x[sl, :] + mod_ref[0, gate_idx:gate_idx + 1, :] * upd[sl, :]


def _matmul_resid(a, b, layer, x, mod, gate_idx, *, row0=0, rows=N_TOK, x_row0=None, merged=None, name):
    tm, tn, tk = 2048, 1024, 1024
    kdim = a.shape[1]
    n = b.shape[-1]
    nk = kdim // tk
    rb0 = row0 // tm
    x_rb0 = rb0 if x_row0 is None else x_row0 // tm
    out_rb0 = 0 if merged is None else rb0
    n_mod = tm // DEC_SEQ
    mod_specs = [pl.BlockSpec((1, 6, tn),
                              lambda i, j, k, p=p: (_mod_row((rb0 + i) * n_mod + p, DEC_SEQ), 0, j))
                 for p in range(n_mod)]
    in_specs = [pl.BlockSpec((tm, tk), lambda i, j, k: (rb0 + i, k)),
                pl.BlockSpec((None, tk, tn), lambda i, j, k: (layer, k, j)),
                pl.BlockSpec((tm, tn), lambda i, j, k: (x_rb0 + i, j))] + mod_specs
    args = [a, b, x] + [mod] * n_mod
    aliases = {} if merged is None or isinstance(merged, str) else _mix_alias(merged, args, in_specs)
    vmem = (2 * (tm * tk * 2 + tk * tn * b.dtype.itemsize + 2 * tm * tn * 4)
            + 2 * tk * tn * 2 + 2 * tm * tk * 2)
    return pl.pallas_call(
        functools.partial(_mm_resid_kernel, nk=nk, n_mod=n_mod, gate_idx=gate_idx),
        out_shape=jax.ShapeDtypeStruct((rows if merged is None else N_TOK, n), F32),
        grid=(rows // tm, n // tn, nk),
        in_specs=in_specs,
        out_specs=pl.BlockSpec((tm, tn), lambda i, j, k: (out_rb0 + i, j)),
        input_output_aliases=aliases,
        compiler_params=_cparams(("arbitrary", "arbitrary", "arbitrary"), vmem),
        name=name,
    )(*args)


def _mm_ws_kernel(a_ref, b_ref, o_ref, *scratch, relu2, b_transposed):
    if scratch:
        bw_ref = scratch[0]

        @pl.when(pl.program_id(1) == 0)
        def _():
            bw_ref[...] = b_ref[...].astype(BF16)

        b = bw_ref[...]
    else:
        b = b_ref[...]
    contract_b = 1 if b_transposed else 0
    r = lax.dot_general(a_ref[...], b, (((1,), (contract_b,)), ((), ())), preferred_element_type=F32)
    if relu2:
        r = jnp.square(jnp.maximum(r, 0.0))
    o_ref[...] = r.astype(o_ref.dtype)


def _matmul_ws(a, b, layer, *, tm, tn, out_dtype, relu2=False, b_transposed=False, name):
    m, kdim = a.shape
    if b_transposed:
        n = b.shape[0]
        b_spec = pl.BlockSpec((tn, kdim), lambda j, i: (j, 0))
    elif layer is None:
        n = b.shape[-1]
        b_spec = pl.BlockSpec((kdim, tn), lambda j, i: (0, j))
    else:
        n = b.shape[-1]
        b_spec = pl.BlockSpec((None, kdim, tn), lambda j, i: (layer, 0, j))
    cast = b.dtype != jnp.dtype(BF16)
    o_bytes = jnp.dtype(out_dtype).itemsize
    vmem = (2 * (tm * kdim * 2 + kdim * tn * b.dtype.itemsize + tm * tn * o_bytes)
            + cast * kdim * tn * 2 + 2 * tm * tn * 4)
    return pl.pallas_call(
        functools.partial(_mm_ws_kernel, relu2=relu2, b_transposed=b_transposed),
        out_shape=jax.ShapeDtypeStruct((m, n), out_dtype),
        grid=(n // tn, m // tm),
        in_specs=[pl.BlockSpec((tm, kdim), lambda j, i: (i, 0)), b_spec],
        out_specs=pl.BlockSpec((tm, tn), lambda j, i: (i, j)),
        scratch_shapes=[pltpu.VMEM((tn, kdim) if b_transposed else (kdim, tn), BF16)] if cast else [],
        compiler_params=_cparams(("parallel", "arbitrary"), vmem),
        name=name,
    )(a, b)


def _rope128(r, cos, sa, sb):
    return r * cos + pltpu.roll(r, 96, 1) * sa + pltpu.roll(r, 32, 1) * sb


def _q_kernel(ql_ref, ga_ref, w_ref, gn_ref, gr_ref, cos_ref, sa_ref, sb_ref, o_ref, qn_ref, *, heads_per_tile):
    @pl.when(pl.program_id(1) == 0)
    def _():
        qn_ref[...] = _rms(ql_ref[...], ga_ref[...], Q_LORA).astype(BF16)

    cos, sa, sb = cos_ref[...], sa_ref[...], sb_ref[...]
    qn = qn_ref[...]
    for hh in range(heads_per_tile):
        c0 = hh * QK_PAD
        acc = jnp.dot(qn, w_ref[:, c0:c0 + QK_PAD], preferred_element_type=F32)
        nope = acc[:, :QK_NOPE]
        rope = acc[:, QK_NOPE:]
        nope = _rms(nope, gn_ref[...], QK_NOPE) * ATTN_SCALE
        rope = _rope128(_rms(rope, gr_ref[...], QK_ROPE), cos, sa, sb) * ATTN_SCALE
        o_ref[:, c0:c0 + QK_NOPE] = nope.astype(BF16)
        o_ref[:, c0 + QK_NOPE:c0 + QK_PAD] = rope.astype(BF16)


def _rope_specs(tm, nargs):
    blocks_per_seq = DEC_SEQ // tm

    def idx(i, *_):
        return (jnp.minimum(i * tm // N_PROMPT, 1), i % blocks_per_seq, 0)

    return [pl.BlockSpec((None, tm, V7X_LANES), idx) for _ in range(nargs)]


def _q_proj(proj, ga, w_qb_p, gn, gr128, rope_tabs):
    tm, tn = 512, 1024
    n = MLA_HEADS * QK_PAD
    return pl.pallas_call(
        functools.partial(_q_kernel, heads_per_tile=tn // QK_PAD),
        out_shape=jax.ShapeDtypeStruct((N_TOK, n), BF16),
        grid=(N_TOK // tm, n // tn),
        in_specs=[
            pl.BlockSpec((tm, Q_LORA), lambda i, j: (i, 0)),
            pl.BlockSpec((1, Q_LORA), lambda i, j: (0, 0)),
            pl.BlockSpec((Q_LORA, tn), lambda i, j: (0, j)),
            pl.BlockSpec((1, QK_NOPE), lambda i, j: (0, 0)),
            pl.BlockSpec((1, V7X_LANES), lambda i, j: (0, 0)),
        ] + _rope_specs(tm, 3),
        out_specs=pl.BlockSpec((tm, tn), lambda i, j: (i, j)),
        scratch_shapes=[pltpu.VMEM((tm, Q_LORA), BF16)],
        compiler_params=_cparams(
            ("parallel", "arbitrary"),
            2 * (tm * Q_LORA * 4 + Q_LORA * tn * 2 + tm * tn * 2) + tm * Q_LORA * 2 + 3 * tm * tn * 4),
        name="mla_q_proj",
    )(proj, ga, w_qb_p, gn, gr128, *rope_tabs)


def _kvnorm_kernel(c_ref, r_ref, gc_ref, gk_ref, cos_ref, sa_ref, sb_ref, ckv_ref, kr_ref, krr_ref):
    ckv_ref[...] = _rms(c_ref[...], gc_ref[...], KV_LORA)
    rn = _rms(r_ref[...], gk_ref[...], QK_ROPE)
    kr_ref[...] = rn
    krr_ref[...] = _rope128(rn, cos_ref[...], sa_ref[...], sb_ref[...]).astype(BF16)


def _kv_norm(proj, gc, gk128, rope_tabs):
    tm = 512
    return pl.pallas_call(
        _kvnorm_kernel,
        out_shape=(jax.ShapeDtypeStruct((N_TOK, KV_LORA), F32),
                   jax.ShapeDtypeStruct((N_TOK, V7X_LANES), F32),
                   jax.ShapeDtypeStruct((N_TOK, V7X_LANES), BF16)),
        grid=(N_TOK // tm,),
        in_specs=[
            pl.BlockSpec((tm, KV_LORA), lambda i: (i, P_KV // KV_LORA)),
            pl.BlockSpec((tm, V7X_LANES), lambda i: (i, P_KROPE // V7X_LANES)),
            pl.BlockSpec((1, KV_LORA), lambda i: (0, 0)),
            pl.BlockSpec((1, V7X_LANES), lambda i: (0, 0)),
        ] + _rope_specs(tm, 3),
        out_specs=(pl.BlockSpec((tm, KV_LORA), lambda i: (i, 0)),
                   pl.BlockSpec((tm, V7X_LANES), lambda i: (i, 0)),
                   pl.BlockSpec((tm, V7X_LANES), lambda i: (i, 0))),
        compiler_params=_cparams(("parallel",), 8 * tm * (KV_LORA + 4 * V7X_LANES) * 4),
        name="mla_kv_norm",
    )(proj, proj, gc, gk128, *rope_tabs)


def _kvup_kernel(own_ref, cache_ref, w_ref, gn_ref, o_ref, *, heads_per_tile, k_tiles, own_blocks):
    c = jnp.where(pl.program_id(0) < own_blocks, own_ref[...], cache_ref[...]).astype(BF16)
    pair = 2 * QK_NOPE

    @pl.when(pl.program_id(1) < k_tiles)
    def _():
        for pp in range(heads_per_tile // 2):
            acc = jnp.dot(c, w_ref[:, pp * pair:(pp + 1) * pair], preferred_element_type=F32)
            for hh in range(2):
                c0 = pp * pair + hh * QK_NOPE
                o_ref[:, c0:c0 + QK_NOPE] = _rms(
                    acc[:, hh * QK_NOPE:(hh + 1) * QK_NOPE], gn_ref[...], QK_NOPE).astype(BF16)

    @pl.when(pl.program_id(1) >= k_tiles)
    def _():
        o_ref[...] = jnp.dot(c, w_ref[...], preferred_element_type=F32).astype(BF16)


def _kv_up(ckv, cache, w_kvb_p, gn):
    tm, tn = 512, 1024
    own_blocks = N_TOK // tm
    m = N_TOK + N_CACHE
    n = 2 * MLA_OUT
    return pl.pallas_call(
        functools.partial(_kvup_kernel, heads_per_tile=tn // QK_NOPE, k_tiles=MLA_OUT // tn,
                          own_blocks=own_blocks),
        out_shape=jax.ShapeDtypeStruct((m, n), BF16),
        grid=(m // tm, n // tn),
        in_specs=[
            pl.BlockSpec((tm, KV_LORA), lambda i, j: (jnp.minimum(i, own_blocks - 1), 0)),
            pl.BlockSpec((tm, KV_LORA), lambda i, j: (jnp.maximum(i - own_blocks, 0), 0)),
            pl.BlockSpec((KV_LORA, tn), lambda i, j: (0, j)),
            pl.BlockSpec((1, QK_NOPE), lambda i, j: (0, 0)),
        ],
        out_specs=pl.BlockSpec((tm, tn), lambda i, j: (i, j)),
        compiler_params=_cparams(
            ("arbitrary", "arbitrary"),
            2 * (2 * tm * KV_LORA * 4 + KV_LORA * tn * 2 + tm * tn * 2) + 3 * tm * tn * 4),
        name="mla_kv_up",
    )(ckv, cache, w_kvb_p, gn)


def _attn_kernel(*refs, n_seg):
    q_ref = refs[0]
    seg_refs = [refs[1 + 3 * s:4 + 3 * s] for s in range(n_seg)]
    g_ref = refs[1 + 3 * n_seg]
    o_ref, oacc_ref = refs[-2:]
    def head_scores(h):
        qh = q_ref[:, h * QK_PAD:(h + 1) * QK_PAD]
        out = []
        for kn_ref, kr_ref, _ in seg_refs:
            kh = jnp.concatenate([kn_ref[:, h * QK_NOPE:(h + 1) * QK_NOPE], kr_ref[...]], axis=1)
            out.append(lax.dot_general(qh, kh, (((1,), (1,)), ((), ())), preferred_element_type=F32))
        return out

    next_scores = head_scores(0)
    for h in range(MLA_HEADS):
        scores = next_scores
        if h + 1 < MLA_HEADS:
            next_scores = head_scores(h + 1)
        m = scores[0].max(axis=-1, keepdims=True)
        for s in scores[1:]:
            m = jnp.maximum(m, s.max(axis=-1, keepdims=True))
        denom = None
        out = None
        for s, (_, _, v_ref) in zip(scores, seg_refs):
            p = jnp.exp(s - m)
            ps = p.sum(axis=-1, keepdims=True)
            pv = jnp.dot(p.astype(BF16), v_ref[:, h * V_HEAD:(h + 1) * V_HEAD], preferred_element_type=F32)
            denom = ps if denom is None else denom + ps
            out = pv if out is None else out + pv
        oacc_ref[:, h * V_HEAD:(h + 1) * V_HEAD] = out / denom
    o_ref[...] = _rms(oacc_ref[...], g_ref[...], MLA_OUT).astype(BF16)


def _mix_alias(mix, args, in_specs, out_idx=0):
    if mix is None:
        return {}
    args.append(mix)
    in_specs.append(pl.BlockSpec(memory_space=pl.ANY))
    return {len(args) - 1: out_idx}


def _attention(q, kv, kr, g_out, mix, *, n_batch, t_q, segs, q_row0):
    tq = 256
    qb = t_q // tq
    in_specs = [pl.BlockSpec((tq, MLA_HEADS * QK_PAD), lambda b, i: (q_row0 // tq + b * qb + i, 0))]
    args = [q]
    vmem = 2 * tq * MLA_HEADS * QK_PAD * 2
    for rows, row0 in segs:
        blk0 = row0 // rows
        in_specs += [
            pl.BlockSpec((rows, MLA_OUT), lambda b, i, blk0=blk0: (blk0 + b, 0)),
            pl.BlockSpec((rows, V7X_LANES), lambda b, i, blk0=blk0: (blk0 + b, 0)),
            pl.BlockSpec((rows, MLA_OUT), lambda b, i, blk0=blk0: (blk0 + b, 1)),
        ]
        args += [kv, kr, kv]
        vmem += 2 * rows * (2 * MLA_OUT + V7X_LANES) * 2 + 12 * tq * rows * 4
    in_specs.append(pl.BlockSpec((1, MLA_OUT), lambda b, i: (0, 0)))
    args.append(g_out)
    vmem += 4 * tq * MLA_OUT * 4
    aliases = _mix_alias(mix, args, in_specs)
    return pl.pallas_call(
        functools.partial(_attn_kernel, n_seg=len(segs)),
        out_shape=jax.ShapeDtypeStruct((N_TOK, MLA_OUT + GLA_OUT), BF16),
        grid=(n_batch, qb),
        in_specs=in_specs,
        out_specs=pl.BlockSpec((tq, MLA_OUT), lambda b, i: (q_row0 // tq + b * qb + i, 0)),
        scratch_shapes=[pltpu.VMEM((tq, MLA_OUT), F32)],
        input_output_aliases=aliases,
        compiler_params=_cparams(("parallel", "arbitrary"), vmem),
        name="mla_attention_%d" % t_q,
    )(*args)


def _log_sigmoid(x):
    return -(jnp.maximum(-x, 0.0) + jnp.log(1.0 + jnp.exp(-jnp.abs(x))))


def _exact_tri_dot(keep, g):
    tri = jnp.where(keep, 1.0, 0.0).astype(BF16)
    g_hi = g.astype(BF16)
    rem = g - g_hi.astype(F32)
    g_mid = rem.astype(BF16)
    g_lo = (rem - g_mid.astype(F32)).astype(BF16)
    return (jnp.dot(tri, g_hi, preferred_element_type=F32)
            + jnp.dot(tri, g_mid, preferred_element_type=F32)
            + jnp.dot(tri, g_lo, preferred_element_type=F32))


def _gla_head_group(t):
    return GLA_HEADS if t <= SEQ else GLA_HEADS // 2


def _gla_kernel(q_ref, k_ref, v_ref, glr_ref, wgf_ref, wgb_ref, bgf_ref, bgb_ref, s0f_ref, s0b_ref,
                og_ref, gn_ref, *rest, n_chunks, hg):
    o_ref, sf_ref, sb_ref, gf_s, gb_s, of_s, ob_s, st_s = rest[-8:]
    glr = glr_ref[...].astype(BF16)

    def gate(w_ref, b_ref):
        x = jnp.dot(glr, w_ref[...].astype(BF16), preferred_element_type=F32) + b_ref[...]
        return _log_sigmoid(x) * (1.0 / GATE_TAU)

    gf_s[...] = gate(wgf_ref, bgf_ref)
    gb_s[...] = gate(wgb_ref, bgb_ref)
    for j in range(hg):
        st_s[j] = s0f_ref[j].T
        st_s[hg + j] = s0b_ref[j].T

    row = lax.broadcasted_iota(jnp.int32, (CHUNK, CHUNK), 0)
    col = lax.broadcasted_iota(jnp.int32, (CHUNK, CHUNK), 1)
    nt = (((1,), (1,)), ((), ()))
    tn = (((0,), (0,)), ((), ()))

    keep_f = col <= row
    keep_b = col >= row

    def cumulate(n, carry):
        sl = pl.ds(pl.multiple_of(n * CHUNK, CHUNK), CHUNK)
        gf_s[sl, :] = _exact_tri_dot(keep_f, gf_s[sl, :])
        gb_s[sl, :] = _exact_tri_dot(keep_b, gb_s[sl, :])
        return carry

    lax.fori_loop(0, n_chunks, cumulate, 0)

    def step(n, carry):
        units = []
        for j in range(hg):
            units.append((n, j, j, gf_s, of_s, keep_f, CHUNK - 1, CHUNK // 2 - 1))
            units.append((n_chunks - 1 - n, j, hg + j, gb_s, ob_s, keep_b, 0, CHUNK // 2))
        staged = []
        for cn, j, slot, b_s, o_s, keep, end_idx, mid_idx in units:
            sl = pl.ds(pl.multiple_of(cn * CHUNK, CHUNK), CHUNK)
            dk = slice(j * GLA_DK, (j + 1) * GLA_DK)
            dv = slice(j * GLA_DV, (j + 1) * GLA_DV)
            b = b_s[sl, dk]
            q = q_ref[sl, dk] * GLA_SCALE
            k = k_ref[sl, dk]
            v = v_ref[sl, dv].astype(BF16)
            b_end = b[end_idx:end_idx + 1, :]
            b_mid = b[mid_idx:mid_idx + 1, :]
            qe = (q * jnp.exp(b)).astype(BF16)
            qa = (q * jnp.exp(b - b_mid)).astype(BF16)
            ka = (k * jnp.exp(b_mid - b)).astype(BF16)
            kd = (k * jnp.exp(b_end - b)).astype(BF16)
            staged.append((sl, dv, slot, o_s, keep, v, qe, qa, ka, kd, jnp.exp(b_end)))
        scores = [lax.dot_general(qa, ka, nt, preferred_element_type=F32)
                  for _, _, _, _, _, _, _, qa, ka, _, _ in staged]
        inter = [lax.dot_general(qe, st_s[slot].astype(BF16), nt, preferred_element_type=F32)
                 for _, _, slot, _, _, _, qe, _, _, _, _ in staged]
        update = [lax.dot_general(v, kd, tn, preferred_element_type=F32)
                  for _, _, _, _, _, v, _, _, _, kd, _ in staged]
        for (sl, dv, slot, o_s, keep, v, _, _, _, _, decay), a, o_inter, upd in zip(staged, scores, inter, update):
            a = jnp.where(keep, a, 0.0).astype(BF16)
            o_s[sl, dv] = jnp.dot(a, v, preferred_element_type=F32) + o_inter
            st_s[slot] = st_s[slot] * decay + upd
        return carry

    lax.fori_loop(0, n_chunks, step, 0)
    for j in range(hg):
        sf_ref[j] = st_s[j].T
        sb_ref[j] = st_s[hg + j].T
        dv = slice(j * GLA_DV, (j + 1) * GLA_DV)
        og = og_ref[:, dv]
        o = _rms(of_s[:, dv] + ob_s[:, dv], gn_ref[...], GLA_DV)
        o_ref[:, dv] = (o * (og / (1.0 + jnp.exp(-og)))).astype(BF16)


def _gla(proj, wg, bg, s0f, s0b, gn, mix, *, n_batch, t, row0, layer=0, n_layers=1, states=(None, None)):
    hg = _gla_head_group(t)
    rb0 = row0 // t
    n_groups = GLA_HEADS // hg
    tok = lambda width, col0: pl.BlockSpec(
        (t, hg * width), lambda b, h, cb=col0 // (hg * width): (rb0 + b, cb + h))
    state = pl.BlockSpec((None, hg, GLA_DK, GLA_DV), lambda b, h: (b, h, 0, 0))
    in_specs = [
        tok(GLA_DK, P_GQ),
        tok(GLA_DK, P_GK),
        tok(GLA_DV, P_GV),
        pl.BlockSpec((t, V7X_LANES), lambda b, h: (rb0 + b, P_GATE // V7X_LANES)),
        pl.BlockSpec((V7X_LANES, hg * GLA_DK), lambda b, h: (0, h)),
        pl.BlockSpec((V7X_LANES, hg * GLA_DK), lambda b, h: (0, n_groups + h)),
        pl.BlockSpec((1, hg * GLA_DK), lambda b, h: (0, h)),
        pl.BlockSpec((1, hg * GLA_DK), lambda b, h: (0, n_groups + h)),
        state, state,
        tok(GLA_DV, P_OG),
        pl.BlockSpec((1, GLA_DV), lambda b, h: (0, 0)),
    ]
    args = [proj, proj, proj, proj, wg, wg, bg, bg, s0f, s0b, proj, gn]
    aliases = _mix_alias(mix, args, in_specs)
    aliases.update(_mix_alias(states[0], args, in_specs, 1))
    aliases.update(_mix_alias(states[1], args, in_specs, 2))
    state_out = pl.BlockSpec((None, None, hg, GLA_DK, GLA_DV), lambda b, h: (b, layer, h, 0, 0))
    state_shape = jax.ShapeDtypeStruct((n_batch, n_layers, GLA_HEADS, GLA_DK, GLA_DV), F32)
    return pl.pallas_call(
        functools.partial(_gla_kernel, n_chunks=t // CHUNK, hg=hg),
        out_shape=(jax.ShapeDtypeStruct((N_TOK, MLA_OUT + GLA_OUT), BF16), state_shape, state_shape),
        grid=(n_batch, n_groups),
        in_specs=in_specs,
        input_output_aliases=aliases,
        out_specs=(pl.BlockSpec((t, hg * GLA_DV),
                                lambda b, h: (rb0 + b, MLA_OUT // (hg * GLA_DV) + h)), state_out, state_out),
        scratch_shapes=[pltpu.VMEM((t, hg * GLA_DK), F32), pltpu.VMEM((t, hg * GLA_DK), F32),
                        pltpu.VMEM((t, hg * GLA_DV), F32), pltpu.VMEM((t, hg * GLA_DV), F32),
                        pltpu.VMEM((2 * hg, GLA_DV, GLA_DK), F32)],
        compiler_params=_cparams(
            ("parallel", "parallel"),
            2 * (t * hg * (2 * GLA_DK + 2 * GLA_DV) * 4 + t * V7X_LANES * 4 + t * hg * GLA_DV * 2)
            + t * hg * (2 * GLA_DK + 2 * GLA_DV) * 4 + 6 * t * hg * GLA_DK * 4
            + 10 * hg * GLA_DK * GLA_DV * 4 + 4 * V7X_LANES * hg * GLA_DK * 2),
        name="gla_%d" % t,
    )(*args)


def _pack_w_in_kernel(w_ref, o_ref):
    segments = ((0, 0, OFF_KV + KV_LORA), (P_KROPE, OFF_KV + KV_LORA, QK_ROPE), (P_GATE, OFF_GATE, 2 * GATE_RANK),
                (P_GQ, OFF_GQ, OFF_GATE - OFF_GQ), (P_OG, OFF_OG, GLA_OUT))
    cols = o_ref.shape[1]
    o_ref[P_KROPE:P_GQ, :] = jnp.zeros((P_GQ - P_KROPE, cols), BF16)
    for dst, src, rows in segments:
        o_ref[dst:dst + rows, :] = w_ref[src:src + rows, :].astype(BF16)


def _pack_w_in(w_in_t, layer):
    tc = 256
    return pl.pallas_call(
        _pack_w_in_kernel,
        out_shape=jax.ShapeDtypeStruct((P_TOTAL, D_MODEL), BF16),
        grid=(D_MODEL // tc,),
        in_specs=[pl.BlockSpec((None, N_IN, tc), lambda i: (layer, 0, i))],
        out_specs=pl.BlockSpec((P_TOTAL, tc), lambda i: (0, i)),
        compiler_params=_cparams(("parallel",), 2 * tc * (N_IN * 4 + P_TOTAL * 2) + 2 * tc * P_TOTAL * 4),
        name="pack_w_in",
    )(w_in_t)


def _pack_w_qb(w):
    w = w.reshape(Q_LORA, MLA_HEADS, QK_NOPE + QK_ROPE)
    w = jnp.pad(w, ((0, 0), (0, 0), (0, QK_PAD - QK_NOPE - QK_ROPE)))
    return w.reshape(Q_LORA, MLA_HEADS * QK_PAD).astype(BF16)


def _pack_w_kvb(w):
    w = w.reshape(KV_LORA, MLA_HEADS, QK_NOPE + V_HEAD)
    return jnp.concatenate([w[:, :, :QK_NOPE].reshape(KV_LORA, MLA_OUT),
                            w[:, :, QK_NOPE:].reshape(KV_LORA, MLA_OUT)], axis=1).astype(BF16)


def _pack_gate(w_gf2, w_gb2, b_gf, b_gb):
    wg = jnp.zeros((V7X_LANES, 2 * GLA_QK), F32)
    wg = wg.at[:GATE_RANK, :GLA_QK].set(w_gf2).at[GATE_RANK:2 * GATE_RANK, GLA_QK:].set(w_gb2)
    return wg.astype(BF16), jnp.concatenate([b_gf, b_gb]).reshape(1, 2 * GLA_QK)


def _pad_lanes(g):
    return jnp.pad(g, (0, V7X_LANES - g.shape[0])).reshape(1, V7X_LANES)


def _rope_tables():
    rows = DEC_SEQ // GRID_W
    row = jnp.repeat(jnp.arange(rows, dtype=F32), GRID_W)
    col = jnp.tile(jnp.arange(GRID_W, dtype=F32), rows)
    n_freq = QK_ROPE // 4
    inv = jnp.power(ROPE_BASE, -jnp.arange(n_freq, dtype=F32) / n_freq)
    ang = jnp.concatenate([row[:, None] * inv, col[:, None] * inv], axis=-1)
    cos, sin = jnp.cos(ang), jnp.sin(ang)
    half = QK_ROPE // 2
    zpad = jnp.zeros((DEC_SEQ, V7X_LANES - QK_ROPE), F32)
    zhalf = jnp.zeros((DEC_SEQ, half), F32)
    cos_t = jnp.concatenate([cos, cos, zpad], axis=1)
    sa_t = jnp.concatenate([-sin, zhalf, zpad], axis=1)
    sb_t = jnp.concatenate([zhalf, sin, zpad], axis=1)
    ident = jnp.concatenate([jnp.ones((DEC_SEQ, QK_ROPE), F32), zpad], axis=1)
    zeros = jnp.zeros((DEC_SEQ, V7X_LANES), F32)
    return (jnp.stack([ident, cos_t]), jnp.stack([zeros, sa_t]), jnp.stack([zeros, sb_t]))


def kernel(x_prompt, x_sample, c, cache_ckv, cache_krope, state_gla_fwd, state_gla_bwd, c_ctx, w_ada, b_ada, norm1, norm2, w_in, q_a_norm, w_qb, kv_a_norm, w_kvb, q_norm_nope, q_norm_rope, k_norm_nope, k_norm_rope, w_gf2, b_gf, w_gb2, b_gb, gla_norm, mla_out_norm, w_o, w_up, w_down):
    x = (x_prompt.reshape(N_PROMPT, D_MODEL), x_sample.reshape(N_SAMPLE, D_MODEL))
    cvec = jnp.concatenate([c_ctx[None, :], c, jnp.zeros((N_MOD_ROWS - 1 - DEC_BATCH, D_MODEL), F32)], axis=0)
    mods = _ada(cvec, w_ada, b_ada).reshape(DEPTH, N_MOD_ROWS, 6, D_MODEL)
    rope_tabs = _rope_tables()
    zero_state = jnp.zeros((BATCH, GLA_HEADS, GLA_DK, GLA_DV), F32)
    w_in_t = jnp.swapaxes(w_in, 1, 2)

    ckvs, kropes = [], []
    state_stacks = (None, None)
    h = _normmod(x, norm1[0], mods[0], 0, 1)
    for l in range(DEPTH):
        mod = mods[l]
        proj = _matmul_ws(h, _pack_w_in(w_in_t, l), None, tm=1024, tn=1024, out_dtype=F32,
                          b_transposed=True, name="in_proj")

        q = _q_proj(proj, q_a_norm[l].reshape(1, Q_LORA), _pack_w_qb(w_qb[l]),
                    q_norm_nope[l].reshape(1, QK_NOPE), _pad_lanes(q_norm_rope[l]), rope_tabs)
        ckv, kr, krr = _kv_norm(proj, kv_a_norm[l].reshape(1, KV_LORA), _pad_lanes(k_norm_rope[l]), rope_tabs)
        kv = _kv_up(ckv, cache_ckv[:, l].reshape(N_CACHE, KV_LORA), _pack_w_kvb(w_kvb[l]),
                    k_norm_nope[l].reshape(1, QK_NOPE))
        kr_cache = jnp.pad(cache_krope[:, l].reshape(N_CACHE, QK_ROPE),
                           ((0, 0), (0, V7X_LANES - QK_ROPE))).astype(BF16)
        kr_all = jnp.concatenate([krr, kr_cache], axis=0)
        g_mla = mla_out_norm[l].reshape(1, MLA_OUT)
        mix = _attention(q, kv, kr_all, g_mla, None, n_batch=BATCH, t_q=SEQ, q_row0=0,
                         segs=[(SEQ, 0)])
        mix = _attention(q, kv, kr_all, g_mla, mix, n_batch=DEC_BATCH, t_q=DEC_SEQ, q_row0=N_PROMPT,
                         segs=[(DEC_SEQ, N_PROMPT), (PAST_LEN, N_TOK)])

        wg, bg = _pack_gate(w_gf2[l], w_gb2[l], b_gf[l], b_gb[l])
        gn = gla_norm[l].reshape(1, GLA_DV)
        mix, *state_stacks = _gla(proj, wg, bg, zero_state, zero_state, gn, mix, n_batch=BATCH, t=SEQ, row0=0,
                                  layer=l, n_layers=DEPTH, states=state_stacks)
        mix, _, _ = _gla(proj, wg, bg, state_gla_fwd[:, l], state_gla_bwd[:, l], gn, mix,
                         n_batch=DEC_BATCH, t=DEC_SEQ, row0=N_PROMPT)

        if isinstance(x, tuple):
            x_ctx, x_lat = x
            x = _matmul_resid(mix, w_o, l, x_ctx, mod, 2, row0=0, rows=N_PROMPT, x_row0=0,
                              merged="new", name="out_proj_ctx")
            x = _matmul_resid(mix, w_o, l, x_lat, mod, 2, row0=N_PROMPT, rows=N_SAMPLE, x_row0=0,
                              merged=x, name="out_proj_lat")
        else:
            x = _matmul_resid(mix, w_o, l, x, mod, 2, name="out_proj")

        h2 = _normmod(x, norm2[l], mod, 3, 4)
        act = _matmul_ws(h2, w_up, l, tm=512, tn=1024, out_dtype=BF16, relu2=True, name="mlp_up")
        if l + 1 < DEPTH:
            x = _matmul_resid(act, w_down, l, x, mod, 5, name="mlp_down")
            h = _normmod(x, norm1[l + 1], mods[l + 1], 0, 1)
        else:
            y_prompt = _matmul_resid(act, w_down, l, x, mod, 5, row0=0, rows=N_PROMPT, name="mlp_down_ctx")
            y_sample = _matmul_resid(act, w_down, l, x, mod, 5, row0=N_PROMPT, rows=N_SAMPLE, name="mlp_down_lat")

        ckvs.append(ckv[:N_PROMPT].reshape(BATCH, SEQ, KV_LORA))
        kropes.append(kr[:N_PROMPT, :QK_ROPE].reshape(BATCH, SEQ, QK_ROPE))

    return (y_prompt.reshape(BATCH, SEQ, D_MODEL),
            y_sample.reshape(DEC_BATCH, DEC_SEQ, D_MODEL),
            jnp.stack(ckvs, axis=1), jnp.stack(kropes, axis=1),
            state_stacks[0], state_stacks[1])
```

```python
import functools

import jax
import jax.numpy as jnp
from jax import lax
from jax.experimental import pallas as pl
from jax.experimental.pallas import tpu as pltpu

D_MODEL = 4096
BATCH = 16
SEQ = 256
DEPTH = 2
DEC_BATCH = 4
DEC_SEQ = 1024
PAST_LEN = 512
GRID_W = 64
MLA_HEADS = 16
QK_NOPE = 128
QK_ROPE = 64
V_HEAD = 128
Q_LORA = 1024
KV_LORA = 512
GLA_HEADS = 8
GLA_DK = 128
GLA_DV = 256
GATE_RANK = 16
GATE_TAU = 16.0
CHUNK = 64
D_FF = 4 * D_MODEL
ROPE_BASE = 10000.0
EPS = 1e-6

MLA_OUT = MLA_HEADS * V_HEAD
GLA_QK = GLA_HEADS * GLA_DK
GLA_OUT = GLA_HEADS * GLA_DV
OFF_KV = Q_LORA
OFF_GQ = OFF_KV + KV_LORA + QK_ROPE
OFF_GK = OFF_GQ + GLA_QK
OFF_GV = OFF_GK + GLA_QK
OFF_GATE = OFF_GV + GLA_OUT
OFF_OG = OFF_GATE + 2 * GATE_RANK
N_IN = OFF_OG + GLA_OUT

F32 = jnp.float32
BF16 = jnp.bfloat16

V7X_LANES = 128
V7X_SCOPED_VMEM_BYTES = 60000 * 1024

N_PROMPT = BATCH * SEQ
N_SAMPLE = DEC_BATCH * DEC_SEQ
N_TOK = N_PROMPT + N_SAMPLE
N_CACHE = DEC_BATCH * PAST_LEN
N_MOD_ROWS = 8

P_KV = 1024
P_KROPE = P_KV + KV_LORA
P_GATE = P_KROPE + V7X_LANES
P_GQ = 2048
P_GK = P_GQ + GLA_QK
P_GV = P_GK + GLA_QK
P_OG = P_GV + GLA_OUT
P_TOTAL = P_OG + GLA_OUT
QK_PAD = 2 * V7X_LANES

ATTN_SCALE = float((QK_NOPE + QK_ROPE) ** -0.5)
GLA_SCALE = float(GLA_DK ** -0.5)


def _cparams(sem, vmem_bytes):
    return pltpu.CompilerParams(dimension_semantics=sem,
                                vmem_limit_bytes=min(int(vmem_bytes), V7X_SCOPED_VMEM_BYTES))


def _mod_row(row_block, rows_per_block):
    return jnp.maximum(row_block * rows_per_block // DEC_SEQ - (N_PROMPT // DEC_SEQ - 1), 0)


def _rms(x, g, n):
    ms = jnp.sum(x * x, axis=-1, keepdims=True) * (1.0 / n)
    return x * lax.rsqrt(ms + EPS) * g


def _ada_kernel(c_ref, w_ref, b_ref, o_ref):
    k = pl.program_id(2)

    @pl.when(k == 0)
    def _():
        o_ref[...] = jnp.broadcast_to(b_ref[...], o_ref.shape)

    c = c_ref[...]
    s = (c / (1.0 + jnp.exp(-c))).astype(BF16)
    o_ref[...] += jnp.dot(s, w_ref[...].astype(BF16), preferred_element_type=F32)


def _ada(cvec, w_ada, b_ada):
    tk, tn = 1024, 2048
    n_out = 6 * D_MODEL
    return pl.pallas_call(
        _ada_kernel,
        out_shape=jax.ShapeDtypeStruct((DEPTH, N_MOD_ROWS, n_out), F32),
        grid=(DEPTH, n_out // tn, D_MODEL // tk),
        in_specs=[
            pl.BlockSpec((N_MOD_ROWS, tk), lambda l, j, k: (0, k)),
            pl.BlockSpec((None, tk, tn), lambda l, j, k: (l, k, j)),
            pl.BlockSpec((None, 1, tn), lambda l, j, k: (l, 0, j)),
        ],
        out_specs=pl.BlockSpec((None, N_MOD_ROWS, tn), lambda l, j, k: (l, 0, j)),
        compiler_params=_cparams(("parallel", "parallel", "arbitrary"), 3 * tk * tn * 4),
        name="ada_mod",
    )(cvec, w_ada, b_ada.reshape(DEPTH, 1, n_out))


ROW_TILE = 512


def _stream_specs(x):
    if not isinstance(x, tuple):
        return [pl.BlockSpec((ROW_TILE, D_MODEL), lambda i: (i, 0))], [x]
    pb = N_PROMPT // ROW_TILE
    return ([pl.BlockSpec((ROW_TILE, D_MODEL), lambda i: (jnp.minimum(i, pb - 1), 0)),
             pl.BlockSpec((ROW_TILE, D_MODEL), lambda i: (jnp.maximum(i - pb, 0), 0))], list(x))


def _load_stream(x_refs):
    if len(x_refs) == 1:
        return x_refs[0][...]
    in_prompt = pl.program_id(0) < N_PROMPT // ROW_TILE
    return jnp.where(in_prompt, x_refs[0][...], x_refs[1][...])


def _modulated_norm(x, g_ref, mod_ref, shift_idx, scale_idx):
    y = _rms(x, g_ref[...], D_MODEL)
    sc = mod_ref[0, scale_idx:scale_idx + 1, :]
    sh = mod_ref[0, shift_idx:shift_idx + 1, :]
    return (y * (1.0 + sc) + sh).astype(BF16)


def _normmod_kernel(*refs, n_x, shift_idx, scale_idx):
    g_ref, mod_ref, o_ref = refs[n_x:]
    o_ref[...] = _modulated_norm(_load_stream(refs[:n_x]), g_ref, mod_ref, shift_idx, scale_idx)


def _normmod(x, g, mod, shift_idx, scale_idx):
    x_specs, x_args = _stream_specs(x)
    row = pl.BlockSpec((ROW_TILE, D_MODEL), lambda i: (i, 0))
    return pl.pallas_call(
        functools.partial(_normmod_kernel, n_x=len(x_args), shift_idx=shift_idx, scale_idx=scale_idx),
        out_shape=jax.ShapeDtypeStruct((N_TOK, D_MODEL), BF16),
        grid=(N_TOK // ROW_TILE,),
        in_specs=x_specs + [
            pl.BlockSpec((1, D_MODEL), lambda i: (0, 0)),
            pl.BlockSpec((1, 6, D_MODEL), lambda i: (_mod_row(i, ROW_TILE), 0, 0)),
        ],
        out_specs=row,
        compiler_params=_cparams(("arbitrary",), 8 * ROW_TILE * D_MODEL * 4),
        name="norm_mod",
    )(*x_args, g.reshape(1, D_MODEL), mod)


def _mm_resid_kernel(a_ref, b_ref, x_ref, *rest, nk, n_mod, gate_idx):
    mod_refs, o_ref = rest[:n_mod], rest[-1]
    k = pl.program_id(2)

    def prod():
        return jnp.dot(a_ref[...], b_ref[...].astype(BF16), preferred_element_type=F32)

    @pl.when(k == 0)
    def _():
        o_ref[...] = prod()

    @pl.when(jnp.logical_and(k > 0, k < nk - 1))
    def _():
        o_ref[...] += prod()

    @pl.when(k == nk - 1)
    def _():
        upd = o_ref[...] + prod()
        x = x_ref[...]
        rows = o_ref.shape[0] // n_mod
        for part, mod_ref in enumerate(mod_refs):
            sl = slice(part * rows, (part + 1) * rows)
            o_ref[sl, :] = x[sl, :] + mod_ref[0, gate_idx:gate_idx + 1, :] * upd[sl, :]


def _matmul_resid(a, b, layer, x, mod, gate_idx, *, row0=0, rows=N_TOK, x_row0=None, merged=None, name):
    tm, tn, tk = 2048, 1024, 1024
    kdim = a.shape[1]
    n = b.shape[-1]
    nk = kdim // tk
    rb0 = row0 // tm
    x_rb0 = rb0 if x_row0 is None else x_row0 // tm
    out_rb0 = 0 if merged is None else rb0
    n_mod = tm // DEC_SEQ
    mod_specs = [pl.BlockSpec((1, 6, tn),
                              lambda i, j, k, p=p: (_mod_row((rb0 + i) * n_mod + p, DEC_SEQ), 0, j))
                 for p in range(n_mod)]
    in_specs = [pl.BlockSpec((tm, tk), lambda i, j, k: (rb0 + i, k)),
                pl.BlockSpec((None, tk, tn), lambda i, j, k: (layer, k, j)),
                pl.BlockSpec((tm, tn), lambda i, j, k: (x_rb0 + i, j))] + mod_specs
    args = [a, b, x] + [mod] * n_mod
    aliases = {} if merged is None or isinstance(merged, str) else _mix_alias(merged, args, in_specs)
    vmem = (2 * (tm * tk * 2 + tk * tn * b.dtype.itemsize + 2 * tm * tn * 4)
            + 2 * tk * tn * 2 + 2 * tm * tk * 2)
    return pl.pallas_call(
        functools.partial(_mm_resid_kernel, nk=nk, n_mod=n_mod, gate_idx=gate_idx),
        out_shape=jax.ShapeDtypeStruct((rows if merged is None else N_TOK, n), F32),
        grid=(rows // tm, n // tn, nk),
        in_specs=in_specs,
        out_specs=pl.BlockSpec((tm, tn), lambda i, j, k: (out_rb0 + i, j)),
        input_output_aliases=aliases,
        compiler_params=_cparams(("arbitrary", "arbitrary", "arbitrary"), vmem),
        name=name,
    )(*args)


def _mm_ws_kernel(a_ref, b_ref, o_ref, *scratch, relu2, b_transposed):
    if scratch:
        bw_ref = scratch[0]

        @pl.when(pl.program_id(1) == 0)
        def _():
            bw_ref[...] = b_ref[...].astype(BF16)

        b = bw_ref[...]
    else:
        b = b_ref[...]
    contract_b = 1 if b_transposed else 0
    r = lax.dot_general(a_ref[...], b, (((1,), (contract_b,)), ((), ())), preferred_element_type=F32)
    if relu2:
        r = jnp.square(jnp.maximum(r, 0.0))
    o_ref[...] = r.astype(o_ref.dtype)


def _matmul_ws(a, b, layer, *, tm, tn, out_dtype, relu2=False, b_transposed=False, name):
    m, kdim = a.shape
    if b_transposed:
        n = b.shape[0]
        b_spec = pl.BlockSpec((tn, kdim), lambda j, i: (j, 0))
    elif layer is None:
        n = b.shape[-1]
        b_spec = pl.BlockSpec((kdim, tn), lambda j, i: (0, j))
    else:
        n = b.shape[-1]
        b_spec = pl.BlockSpec((None, kdim, tn), lambda j, i: (layer, 0, j))
    cast = b.dtype != jnp.dtype(BF16)
    o_bytes = jnp.dtype(out_dtype).itemsize
    vmem = (2 * (tm * kdim * 2 + kdim * tn * b.dtype.itemsize + tm * tn * o_bytes)
            + cast * kdim * tn * 2 + 2 * tm * tn * 4)
    return pl.pallas_call(
        functools.partial(_mm_ws_kernel, relu2=relu2, b_transposed=b_transposed),
        out_shape=jax.ShapeDtypeStruct((m, n), out_dtype),
        grid=(n // tn, m // tm),
        in_specs=[pl.BlockSpec((tm, kdim), lambda j, i: (i, 0)), b_spec],
        out_specs=pl.BlockSpec((tm, tn), lambda j, i: (i, j)),
        scratch_shapes=[pltpu.VMEM((tn, kdim) if b_transposed else (kdim, tn), BF16)] if cast else [],
        compiler_params=_cparams(("parallel", "arbitrary"), vmem),
        name=name,
    )(a, b)


def _rope128(r, cos, sa, sb):
    return r * cos + pltpu.roll(r, 96, 1) * sa + pltpu.roll(r, 32, 1) * sb


def _q_kernel(ql_ref, ga_ref, w_ref, gn_ref, gr_ref, cos_ref, sa_ref, sb_ref, o_ref, qn_ref, *, heads_per_tile):
    @pl.when(pl.program_id(1) == 0)
    def _():
        qn_ref[...] = _rms(ql_ref[...], ga_ref[...], Q_LORA).astype(BF16)

    cos, sa, sb = cos_ref[...], sa_ref[...], sb_ref[...]
    qn = qn_ref[...]
    for hh in range(heads_per_tile):
        c0 = hh * QK_PAD
        acc = jnp.dot(qn, w_ref[:, c0:c0 + QK_PAD], preferred_element_type=F32)
        nope = acc[:, :QK_NOPE]
        rope = acc[:, QK_NOPE:]
        nope = _rms(nope, gn_ref[...], QK_NOPE) * ATTN_SCALE
        rope = _rope128(_rms(rope, gr_ref[...], QK_ROPE), cos, sa, sb) * ATTN_SCALE
        o_ref[:, c0:c0 + QK_NOPE] = nope.astype(BF16)
        o_ref[:, c0 + QK_NOPE:c0 + QK_PAD] = rope.astype(BF16)


def _rope_specs(tm, nargs):
    blocks_per_seq = DEC_SEQ // tm

    def idx(i, *_):
        return (jnp.minimum(i * tm // N_PROMPT, 1), i % blocks_per_seq, 0)

    return [pl.BlockSpec((None, tm, V7X_LANES), idx) for _ in range(nargs)]


def _q_proj(proj, ga, w_qb_p, gn, gr128, rope_tabs):
    tm, tn = 512, 1024
    n = MLA_HEADS * QK_PAD
    return pl.pallas_call(
        functools.partial(_q_kernel, heads_per_tile=tn // QK_PAD),
        out_shape=jax.ShapeDtypeStruct((N_TOK, n), BF16),
        grid=(N_TOK // tm, n // tn),
        in_specs=[
            pl.BlockSpec((tm, Q_LORA), lambda i, j: (i, 0)),
            pl.BlockSpec((1, Q_LORA), lambda i, j: (0, 0)),
            pl.BlockSpec((Q_LORA, tn), lambda i, j: (0, j)),
            pl.BlockSpec((1, QK_NOPE), lambda i, j: (0, 0)),
            pl.BlockSpec((1, V7X_LANES), lambda i, j: (0, 0)),
        ] + _rope_specs(tm, 3),
        out_specs=pl.BlockSpec((tm, tn), lambda i, j: (i, j)),
        scratch_shapes=[pltpu.VMEM((tm, Q_LORA), BF16)],
        compiler_params=_cparams(
            ("parallel", "arbitrary"),
            2 * (tm * Q_LORA * 4 + Q_LORA * tn * 2 + tm * tn * 2) + tm * Q_LORA * 2 + 3 * tm * tn * 4),
        name="mla_q_proj",
    )(proj, ga, w_qb_p, gn, gr128, *rope_tabs)


def _kvnorm_kernel(c_ref, r_ref, gc_ref, gk_ref, cos_ref, sa_ref, sb_ref, *rest, prompt_blocks):
    ckv_ref, krr_ref, ckv_stack_ref, kr_stack_ref = rest[-4:]
    c = _rms(c_ref[...], gc_ref[...], KV_LORA)
    rn = _rms(r_ref[...], gk_ref[...], QK_ROPE)
    ckv_ref[...] = c
    krr_ref[...] = _rope128(rn, cos_ref[...], sa_ref[...], sb_ref[...]).astype(BF16)

    @pl.when(pl.program_id(0) < prompt_blocks)
    def _():
        ckv_stack_ref[...] = c.reshape(ckv_stack_ref.shape)
        kr_stack_ref[...] = rn[:, :QK_ROPE].reshape(kr_stack_ref.shape)


def _kv_norm(proj, gc, gk128, rope_tabs, layer, stacks):
    tm = 512
    pb = N_PROMPT // tm
    per = tm // SEQ
    in_specs = [
        pl.BlockSpec((tm, KV_LORA), lambda i: (i, P_KV // KV_LORA)),
        pl.BlockSpec((tm, V7X_LANES), lambda i: (i, P_KROPE // V7X_LANES)),
        pl.BlockSpec((1, KV_LORA), lambda i: (0, 0)),
        pl.BlockSpec((1, V7X_LANES), lambda i: (0, 0)),
    ] + _rope_specs(tm, 3)
    args = [proj, proj, gc, gk128, *rope_tabs]
    aliases = _mix_alias(stacks[0], args, in_specs, 2)
    aliases.update(_mix_alias(stacks[1], args, in_specs, 3))
    stack = lambda width: pl.BlockSpec((per, None, SEQ, width), lambda i: (jnp.minimum(i, pb - 1), layer, 0, 0))
    return pl.pallas_call(
        functools.partial(_kvnorm_kernel, prompt_blocks=pb),
        out_shape=(jax.ShapeDtypeStruct((N_TOK, KV_LORA), F32),
                   jax.ShapeDtypeStruct((N_TOK, V7X_LANES), BF16),
                   jax.ShapeDtypeStruct((BATCH, DEPTH, SEQ, KV_LORA), F32),
                   jax.ShapeDtypeStruct((BATCH, DEPTH, SEQ, QK_ROPE), F32)),
        grid=(N_TOK // tm,),
        in_specs=in_specs,
        out_specs=(pl.BlockSpec((tm, KV_LORA), lambda i: (i, 0)),
                   pl.BlockSpec((tm, V7X_LANES), lambda i: (i, 0)),
                   stack(KV_LORA), stack(QK_ROPE)),
        input_output_aliases=aliases,
        compiler_params=_cparams(("arbitrary",), 10 * tm * (KV_LORA + 4 * V7X_LANES) * 4),
        name="mla_kv_norm",
    )(*args)


def _kvup_kernel(own_ref, cache_ref, w_ref, gn_ref, o_ref, *, heads_per_tile, k_tiles, own_blocks):
    c = jnp.where(pl.program_id(0) < own_blocks, own_ref[...], cache_ref[...]).astype(BF16)
    pair = 2 * QK_NOPE

    @pl.when(pl.program_id(1) < k_tiles)
    def _():
        for pp in range(heads_per_tile // 2):
            acc = jnp.dot(c, w_ref[:, pp * pair:(pp + 1) * pair], preferred_element_type=F32)
            for hh in range(2):
                c0 = pp * pair + hh * QK_NOPE
                o_ref[:, c0:c0 + QK_NOPE] = _rms(
                    acc[:, hh * QK_NOPE:(hh + 1) * QK_NOPE], gn_ref[...], QK_NOPE).astype(BF16)

    @pl.when(pl.program_id(1) >= k_tiles)
    def _():
        o_ref[...] = jnp.dot(c, w_ref[...], preferred_element_type=F32).astype(BF16)


def _kv_up(ckv, cache, w_kvb_p, gn):
    tm, tn = 512, 1024
    own_blocks = N_TOK // tm
    m = N_TOK + N_CACHE
    n = 2 * MLA_OUT
    return pl.pallas_call(
        functools.partial(_kvup_kernel, heads_per_tile=tn // QK_NOPE, k_tiles=MLA_OUT // tn,
                          own_blocks=own_blocks),
        out_shape=jax.ShapeDtypeStruct((m, n), BF16),
        grid=(m // tm, n // tn),
        in_specs=[
            pl.BlockSpec((tm, KV_LORA), lambda i, j: (jnp.minimum(i, own_blocks - 1), 0)),
            pl.BlockSpec((tm, KV_LORA), lambda i, j: (jnp.maximum(i - own_blocks, 0), 0)),
            pl.BlockSpec((KV_LORA, tn), lambda i, j: (0, j)),
            pl.BlockSpec((1, QK_NOPE), lambda i, j: (0, 0)),
        ],
        out_specs=pl.BlockSpec((tm, tn), lambda i, j: (i, j)),
        compiler_params=_cparams(
            ("arbitrary", "arbitrary"),
            2 * (2 * tm * KV_LORA * 4 + KV_LORA * tn * 2 + tm * tn * 2) + 3 * tm * tn * 4),
        name="mla_kv_up",
    )(ckv, cache, w_kvb_p, gn)


def _attn_kernel(*refs, n_seg):
    q_ref = refs[0]
    seg_refs = [refs[1 + 3 * s:4 + 3 * s] for s in range(n_seg)]
    g_ref = refs[1 + 3 * n_seg]
    o_ref, oacc_ref = refs[-2:]
    def head_scores(h):
        qh = q_ref[:, h * QK_PAD:(h + 1) * QK_PAD]
        out = []
        for kn_ref, kr_ref, _ in seg_refs:
            kh = jnp.concatenate([kn_ref[:, h * QK_NOPE:(h + 1) * QK_NOPE], kr_ref[...]], axis=1)
            out.append(lax.dot_general(qh, kh, (((1,), (1,)), ((), ())), preferred_element_type=F32))
        return out

    next_scores = head_scores(0)
    for h in range(MLA_HEADS):
        scores = next_scores
        if h + 1 < MLA_HEADS:
            next_scores = head_scores(h + 1)
        m = scores[0].max(axis=-1, keepdims=True)
        for s in scores[1:]:
            m = jnp.maximum(m, s.max(axis=-1, keepdims=True))
        denom = None
        out = None
        for s, (_, _, v_ref) in zip(scores, seg_refs):
            p = jnp.exp(s - m)
            ps = p.sum(axis=-1, keepdims=True)
            pv = jnp.dot(p.astype(BF16), v_ref[:, h * V_HEAD:(h + 1) * V_HEAD], preferred_element_type=F32)
            denom = ps if denom is None else denom + ps
            out = pv if out is None else out + pv
        oacc_ref[:, h * V_HEAD:(h + 1) * V_HEAD] = out / denom
    o_ref[...] = _rms(oacc_ref[...], g_ref[...], MLA_OUT).astype(BF16)


def _mix_alias(mix, args, in_specs, out_idx=0):
    if mix is None:
        return {}
    args.append(mix)
    in_specs.append(pl.BlockSpec(memory_space=pl.ANY))
    return {len(args) - 1: out_idx}


def _attention(q, kv, kr, g_out, mix, *, n_batch, t_q, segs, q_row0):
    tq = 256
    qb = t_q // tq
    in_specs = [pl.BlockSpec((tq, MLA_HEADS * QK_PAD), lambda b, i: (q_row0 // tq + b * qb + i, 0))]
    args = [q]
    vmem = 2 * tq * MLA_HEADS * QK_PAD * 2
    for rows, row0 in segs:
        blk0 = row0 // rows
        in_specs += [
            pl.BlockSpec((rows, MLA_OUT), lambda b, i, blk0=blk0: (blk0 + b, 0)),
            pl.BlockSpec((rows, V7X_LANES), lambda b, i, blk0=blk0: (blk0 + b, 0)),
            pl.BlockSpec((rows, MLA_OUT), lambda b, i, blk0=blk0: (blk0 + b, 1)),
        ]
        args += [kv, kr, kv]
        vmem += 2 * rows * (2 * MLA_OUT + V7X_LANES) * 2 + 12 * tq * rows * 4
    in_specs.append(pl.BlockSpec((1, MLA_OUT), lambda b, i: (0, 0)))
    args.append(g_out)
    vmem += 4 * tq * MLA_OUT * 4
    aliases = _mix_alias(mix, args, in_specs)
    return pl.pallas_call(
        functools.partial(_attn_kernel, n_seg=len(segs)),
        out_shape=jax.ShapeDtypeStruct((N_TOK, MLA_OUT + GLA_OUT), BF16),
        grid=(n_batch, qb),
        in_specs=in_specs,
        out_specs=pl.BlockSpec((tq, MLA_OUT), lambda b, i: (q_row0 // tq + b * qb + i, 0)),
        scratch_shapes=[pltpu.VMEM((tq, MLA_OUT), F32)],
        input_output_aliases=aliases,
        compiler_params=_cparams(("parallel", "arbitrary"), vmem),
        name="mla_attention_%d" % t_q,
    )(*args)


def _log_sigmoid(x):
    return -(jnp.maximum(-x, 0.0) + jnp.log(1.0 + jnp.exp(-jnp.abs(x))))


def _exact_tri_dot(keep, g):
    tri = jnp.where(keep, 1.0, 0.0).astype(BF16)
    g_hi = g.astype(BF16)
    rem = g - g_hi.astype(F32)
    g_mid = rem.astype(BF16)
    g_lo = (rem - g_mid.astype(F32)).astype(BF16)
    return (jnp.dot(tri, g_hi, preferred_element_type=F32)
            + jnp.dot(tri, g_mid, preferred_element_type=F32)
            + jnp.dot(tri, g_lo, preferred_element_type=F32))


def _gla_head_group(t):
    return GLA_HEADS if t <= SEQ else GLA_HEADS // 2


def _gla_kernel(q_ref, k_ref, v_ref, glr_ref, wgf_ref, wgb_ref, bgf_ref, bgb_ref, s0f_ref, s0b_ref,
                og_ref, gn_ref, *rest, n_chunks, hg):
    o_ref, sf_ref, sb_ref, gf_s, gb_s, of_s, ob_s, st_s = rest[-8:]
    glr = glr_ref[...].astype(BF16)

    def gate(w_ref, b_ref):
        x = jnp.dot(glr, w_ref[...].astype(BF16), preferred_element_type=F32) + b_ref[...]
        return _log_sigmoid(x) * (1.0 / GATE_TAU)

    gf_s[...] = gate(wgf_ref, bgf_ref)
    gb_s[...] = gate(wgb_ref, bgb_ref)
    for j in range(hg):
        st_s[j] = s0f_ref[j].T
        st_s[hg + j] = s0b_ref[j].T

    row = lax.broadcasted_iota(jnp.int32, (CHUNK, CHUNK), 0)
    col = lax.broadcasted_iota(jnp.int32, (CHUNK, CHUNK), 1)
    nt = (((1,), (1,)), ((), ()))
    tn = (((0,), (0,)), ((), ()))

    keep_f = col <= row
    keep_b = col >= row

    def cumulate(n, carry):
        sl = pl.ds(pl.multiple_of(n * CHUNK, CHUNK), CHUNK)
        gf_s[sl, :] = _exact_tri_dot(keep_f, gf_s[sl, :])
        gb_s[sl, :] = _exact_tri_dot(keep_b, gb_s[sl, :])
        return carry

    lax.fori_loop(0, n_chunks, cumulate, 0)

    def step(n, carry):
        units = []
        for j in range(hg):
            units.append((n, j, j, gf_s, of_s, keep_f, CHUNK - 1, CHUNK // 2 - 1))
            units.append((n_chunks - 1 - n, j, hg + j, gb_s, ob_s, keep_b, 0, CHUNK // 2))
        staged = []
        for cn, j, slot, b_s, o_s, keep, end_idx, mid_idx in units:
            sl = pl.ds(pl.multiple_of(cn * CHUNK, CHUNK), CHUNK)
            dk = slice(j * GLA_DK, (j + 1) * GLA_DK)
            dv = slice(j * GLA_DV, (j + 1) * GLA_DV)
            b = b_s[sl, dk]
            q = q_ref[sl, dk] * GLA_SCALE
            k = k_ref[sl, dk]
            v = v_ref[sl, dv].astype(BF16)
            b_end = b[end_idx:end_idx + 1, :]
            b_mid = b[mid_idx:mid_idx + 1, :]
            qe = (q * jnp.exp(b)).astype(BF16)
            qa = (q * jnp.exp(b - b_mid)).astype(BF16)
            ka = (k * jnp.exp(b_mid - b)).astype(BF16)
            kd = (k * jnp.exp(b_end - b)).astype(BF16)
            staged.append((sl, dv, slot, o_s, keep, v, qe, qa, ka, kd, jnp.exp(b_end)))
        scores = [lax.dot_general(qa, ka, nt, preferred_element_type=F32)
                  for _, _, _, _, _, _, _, qa, ka, _, _ in staged]
        inter = [lax.dot_general(qe, st_s[slot].astype(BF16), nt, preferred_element_type=F32)
                 for _, _, slot, _, _, _, qe, _, _, _, _ in staged]
        update = [lax.dot_general(v, kd, tn, preferred_element_type=F32)
                  for _, _, _, _, _, v, _, _, _, kd, _ in staged]
        for (sl, dv, slot, o_s, keep, v, _, _, _, _, decay), a, o_inter, upd in zip(staged, scores, inter, update):
            a = jnp.where(keep, a, 0.0).astype(BF16)
            o_s[sl, dv] = jnp.dot(a, v, preferred_element_type=F32) + o_inter
            st_s[slot] = st_s[slot] * decay + upd
        return carry

    lax.fori_loop(0, n_chunks, step, 0)
    for j in range(hg):
        sf_ref[j] = st_s[j].T
        sb_ref[j] = st_s[hg + j].T
        dv = slice(j * GLA_DV, (j + 1) * GLA_DV)
        og = og_ref[:, dv]
        o = _rms(of_s[:, dv] + ob_s[:, dv], gn_ref[...], GLA_DV)
        o_ref[:, dv] = (o * (og / (1.0 + jnp.exp(-og)))).astype(BF16)


def _gla(proj, wg, bg, s0f, s0b, gn, mix, *, n_batch, t, row0, layer=0, n_layers=1, states=(None, None)):
    hg = _gla_head_group(t)
    rb0 = row0 // t
    n_groups = GLA_HEADS // hg
    tok = lambda width, col0: pl.BlockSpec(
        (t, hg * width), lambda b, h, cb=col0 // (hg * width): (rb0 + b, cb + h))
    state = pl.BlockSpec((None, hg, GLA_DK, GLA_DV), lambda b, h: (b, h, 0, 0))
    in_specs = [
        tok(GLA_DK, P_GQ),
        tok(GLA_DK, P_GK),
        tok(GLA_DV, P_GV),
        pl.BlockSpec((t, V7X_LANES), lambda b, h: (rb0 + b, P_GATE // V7X_LANES)),
        pl.BlockSpec((V7X_LANES, hg * GLA_DK), lambda b, h: (0, h)),
        pl.BlockSpec((V7X_LANES, hg * GLA_DK), lambda b, h: (0, n_groups + h)),
        pl.BlockSpec((1, hg * GLA_DK), lambda b, h: (0, h)),
        pl.BlockSpec((1, hg * GLA_DK), lambda b, h: (0, n_groups + h)),
        state, state,
        tok(GLA_DV, P_OG),
        pl.BlockSpec((1, GLA_DV), lambda b, h: (0, 0)),
    ]
    args = [proj, proj, proj, proj, wg, wg, bg, bg, s0f, s0b, proj, gn]
    aliases = _mix_alias(mix, args, in_specs)
    aliases.update(_mix_alias(states[0], args, in_specs, 1))
    aliases.update(_mix_alias(states[1], args, in_specs, 2))
    state_out = pl.BlockSpec((None, None, hg, GLA_DK, GLA_DV), lambda b, h: (b, layer, h, 0, 0))
    state_shape = jax.ShapeDtypeStruct((n_batch, n_layers, GLA_HEADS, GLA_DK, GLA_DV), F32)
    return pl.pallas_call(
        functools.partial(_gla_kernel, n_chunks=t // CHUNK, hg=hg),
        out_shape=(jax.ShapeDtypeStruct((N_TOK, MLA_OUT + GLA_OUT), BF16), state_shape, state_shape),
        grid=(n_batch, n_groups),
        in_specs=in_specs,
        input_output_aliases=aliases,
        out_specs=(pl.BlockSpec((t, hg * GLA_DV),
                                lambda b, h: (rb0 + b, MLA_OUT // (hg * GLA_DV) + h)), state_out, state_out),
        scratch_shapes=[pltpu.VMEM((t, hg * GLA_DK), F32), pltpu.VMEM((t, hg * GLA_DK), F32),
                        pltpu.VMEM((t, hg * GLA_DV), F32), pltpu.VMEM((t, hg * GLA_DV), F32),
                        pltpu.VMEM((2 * hg, GLA_DV, GLA_DK), F32)],
        compiler_params=_cparams(
            ("parallel", "parallel"),
            2 * (t * hg * (2 * GLA_DK + 2 * GLA_DV) * 4 + t * V7X_LANES * 4 + t * hg * GLA_DV * 2)
            + t * hg * (2 * GLA_DK + 2 * GLA_DV) * 4 + 6 * t * hg * GLA_DK * 4
            + 10 * hg * GLA_DK * GLA_DV * 4 + 4 * V7X_LANES * hg * GLA_DK * 2),
        name="gla_%d" % t,
    )(*args)


def _pack_w_in_kernel(w_ref, o_ref):
    segments = ((0, 0, OFF_KV + KV_LORA), (P_KROPE, OFF_KV + KV_LORA, QK_ROPE), (P_GATE, OFF_GATE, 2 * GATE_RANK),
                (P_GQ, OFF_GQ, OFF_GATE - OFF_GQ), (P_OG, OFF_OG, GLA_OUT))
    cols = o_ref.shape[1]
    o_ref[P_KROPE:P_GQ, :] = jnp.zeros((P_GQ - P_KROPE, cols), BF16)
    for dst, src, rows in segments:
        o_ref[dst:dst + rows, :] = w_ref[src:src + rows, :].astype(BF16)


def _pack_w_in(w_in_t, layer):
    tc = 256
    return pl.pallas_call(
        _pack_w_in_kernel,
        out_shape=jax.ShapeDtypeStruct((P_TOTAL, D_MODEL), BF16),
        grid=(D_MODEL // tc,),
        in_specs=[pl.BlockSpec((None, N_IN, tc), lambda i: (layer, 0, i))],
        out_specs=pl.BlockSpec((P_TOTAL, tc), lambda i: (0, i)),
        compiler_params=_cparams(("parallel",), 2 * tc * (N_IN * 4 + P_TOTAL * 2) + 2 * tc * P_TOTAL * 4),
        name="pack_w_in",
    )(w_in_t)


def _pack_w_qb(w):
    w = w.reshape(Q_LORA, MLA_HEADS, QK_NOPE + QK_ROPE)
    w = jnp.pad(w, ((0, 0), (0, 0), (0, QK_PAD - QK_NOPE - QK_ROPE)))
    return w.reshape(Q_LORA, MLA_HEADS * QK_PAD).astype(BF16)


def _pack_w_kvb(w):
    w = w.reshape(KV_LORA, MLA_HEADS, QK_NOPE + V_HEAD)
    return jnp.concatenate([w[:, :, :QK_NOPE].reshape(KV_LORA, MLA_OUT),
                            w[:, :, QK_NOPE:].reshape(KV_LORA, MLA_OUT)], axis=1).astype(BF16)


def _pack_gate(w_gf2, w_gb2, b_gf, b_gb):
    wg = jnp.zeros((V7X_LANES, 2 * GLA_QK), F32)
    wg = wg.at[:GATE_RANK, :GLA_QK].set(w_gf2).at[GATE_RANK:2 * GATE_RANK, GLA_QK:].set(w_gb2)
    return wg.astype(BF16), jnp.concatenate([b_gf, b_gb]).reshape(1, 2 * GLA_QK)


def _pad_lanes(g):
    return jnp.pad(g, (0, V7X_LANES - g.shape[0])).reshape(1, V7X_LANES)


def _rope_tables():
    rows = DEC_SEQ // GRID_W
    row = jnp.repeat(jnp.arange(rows, dtype=F32), GRID_W)
    col = jnp.tile(jnp.arange(GRID_W, dtype=F32), rows)
    n_freq = QK_ROPE // 4
    inv = jnp.power(ROPE_BASE, -jnp.arange(n_freq, dtype=F32) / n_freq)
    ang = jnp.concatenate([row[:, None] * inv, col[:, None] * inv], axis=-1)
    cos, sin = jnp.cos(ang), jnp.sin(ang)
    half = QK_ROPE // 2
    zpad = jnp.zeros((DEC_SEQ, V7X_LANES - QK_ROPE), F32)
    zhalf = jnp.zeros((DEC_SEQ, half), F32)
    cos_t = jnp.concatenate([cos, cos, zpad], axis=1)
    sa_t = jnp.concatenate([-sin, zhalf, zpad], axis=1)
    sb_t = jnp.concatenate([zhalf, sin, zpad], axis=1)
    ident = jnp.concatenate([jnp.ones((DEC_SEQ, QK_ROPE), F32), zpad], axis=1)
    zeros = jnp.zeros((DEC_SEQ, V7X_LANES), F32)
    return (jnp.stack([ident, cos_t]), jnp.stack([zeros, sa_t]), jnp.stack([zeros, sb_t]))


def kernel(x_prompt, x_sample, c, cache_ckv, cache_krope, state_gla_fwd, state_gla_bwd, c_ctx, w_ada, b_ada, norm1, norm2, w_in, q_a_norm, w_qb, kv_a_norm, w_kvb, q_norm_nope, q_norm_rope, k_norm_nope, k_norm_rope, w_gf2, b_gf, w_gb2, b_gb, gla_norm, mla_out_norm, w_o, w_up, w_down):
    x = (x_prompt.reshape(N_PROMPT, D_MODEL), x_sample.reshape(N_SAMPLE, D_MODEL))
    cvec = jnp.concatenate([c_ctx[None, :], c, jnp.zeros((N_MOD_ROWS - 1 - DEC_BATCH, D_MODEL), F32)], axis=0)
    mods = _ada(cvec, w_ada, b_ada).reshape(DEPTH, N_MOD_ROWS, 6, D_MODEL)
    rope_tabs = _rope_tables()
    zero_state = jnp.zeros((BATCH, GLA_HEADS, GLA_DK, GLA_DV), F32)
    w_in_t = jnp.swapaxes(w_in, 1, 2)

    kv_stacks = (None, None)
    state_stacks = (None, None)
    h = _normmod(x, norm1[0], mods[0], 0, 1)
    for l in range(DEPTH):
        mod = mods[l]
        proj = _matmul_ws(h, _pack_w_in(w_in_t, l), None, tm=1024, tn=1024, out_dtype=F32,
                          b_transposed=True, name="in_proj")

        q = _q_proj(proj, q_a_norm[l].reshape(1, Q_LORA), _pack_w_qb(w_qb[l]),
                    q_norm_nope[l].reshape(1, QK_NOPE), _pad_lanes(q_norm_rope[l]), rope_tabs)
        ckv, krr, *kv_stacks = _kv_norm(proj, kv_a_norm[l].reshape(1, KV_LORA), _pad_lanes(k_norm_rope[l]),
                                        rope_tabs, l, kv_stacks)
        kv = _kv_up(ckv, cache_ckv[:, l].reshape(N_CACHE, KV_LORA), _pack_w_kvb(w_kvb[l]),
                    k_norm_nope[l].reshape(1, QK_NOPE))
        kr_cache = jnp.pad(cache_krope[:, l].reshape(N_CACHE, QK_ROPE),
                           ((0, 0), (0, V7X_LANES - QK_ROPE))).astype(BF16)
        kr_all = jnp.concatenate([krr, kr_cache], axis=0)
        g_mla = mla_out_norm[l].reshape(1, MLA_OUT)
        mix = _attention(q, kv, kr_all, g_mla, None, n_batch=BATCH, t_q=SEQ, q_row0=0,
                         segs=[(SEQ, 0)])
        mix = _attention(q, kv, kr_all, g_mla, mix, n_batch=DEC_BATCH, t_q=DEC_SEQ, q_row0=N_PROMPT,
                         segs=[(DEC_SEQ, N_PROMPT), (PAST_LEN, N_TOK)])

        wg, bg = _pack_gate(w_gf2[l], w_gb2[l], b_gf[l], b_gb[l])
        gn = gla_norm[l].reshape(1, GLA_DV)
        mix, *state_stacks = _gla(proj, wg, bg, zero_state, zero_state, gn, mix, n_batch=BATCH, t=SEQ, row0=0,
                                  layer=l, n_layers=DEPTH, states=state_stacks)
        mix, _, _ = _gla(proj, wg, bg, state_gla_fwd[:, l], state_gla_bwd[:, l], gn, mix,
                         n_batch=DEC_BATCH, t=DEC_SEQ, row0=N_PROMPT)

        if isinstance(x, tuple):
            x_ctx, x_lat = x
            x = _matmul_resid(mix, w_o, l, x_ctx, mod, 2, row0=0, rows=N_PROMPT, x_row0=0,
                              merged="new", name="out_proj_ctx")
            x = _matmul_resid(mix, w_o, l, x_lat, mod, 2, row0=N_PROMPT, rows=N_SAMPLE, x_row0=0,
                              merged=x, name="out_proj_lat")
        else:
            x = _matmul_resid(mix, w_o, l, x, mod, 2, name="out_proj")

        h2 = _normmod(x, norm2[l], mod, 3, 4)
        act = _matmul_ws(h2, w_up, l, tm=512, tn=1024, out_dtype=BF16, relu2=True, name="mlp_up")
        if l + 1 < DEPTH:
            x = _matmul_resid(act, w_down, l, x, mod, 5, name="mlp_down")
            h = _normmod(x, norm1[l + 1], mods[l + 1], 0, 1)
        else:
            y_prompt = _matmul_resid(act, w_down, l, x, mod, 5, row0=0, rows=N_PROMPT, name="mlp_down_ctx")
            y_sample = _matmul_resid(act, w_down, l, x, mod, 5, row0=N_PROMPT, rows=N_SAMPLE, name="mlp_down_lat")

    return (y_prompt.reshape(BATCH, SEQ, D_MODEL),
            y_sample.reshape(DEC_BATCH, DEC_SEQ, D_MODEL),
            kv_stacks[0], kv_stacks[1],
            state_stacks[0], state_stacks[1])
```
